```python
import jax, jax.numpy as jnp
from jax import lax
import numpy as np

D_MODEL = 1024
BATCH = 8
SEQ = 4096
DEPTH = 1

PLE_DIM = 256
D_FF = 2816
D_MIX = D_MODEL
CHUNK = 128
GM_HEADS = 4
GM_HEAD_DIM = 128
GM_WIDTH = GM_HEADS * GM_HEAD_DIM
SB_HEADS = 8
SB_HEAD_DIM = 64
SB_WIDTH = SB_HEADS * SB_HEAD_DIM
SB_BLOCK = 128
MIX_IN_WIDTH = 2 * GM_WIDTH + 3 * SB_WIDTH
EPS = 1e-6

kernel_name = "hybrid_gmlp_stickbreaking_macaron_block"


def rms_norm(x, g):
    xf = x.astype(jnp.float32)
    y = xf * lax.rsqrt(jnp.mean(xf * xf, axis=-1, keepdims=True) + EPS)
    return (y * g.astype(jnp.float32)).astype(x.dtype)


def swiglu(x, w_in, w_out):
    gate, up = jnp.split(x @ w_in, 2, axis=-1)
    return (jax.nn.silu(gate) * up) @ w_out


def chunked_gmlp(u, v, v_gain, w_s, b_s):
    B, S, _ = u.shape
    nc = S // CHUNK
    vn = rms_norm(v, v_gain).reshape(B, nc, CHUNK, GM_HEADS, GM_HEAD_DIM)
    causal = jnp.tril(jnp.ones((CHUNK, CHUNK), dtype=bool))
    w = jnp.where(causal[None], w_s, jnp.zeros_like(w_s)).astype(vn.dtype)
    sv = jnp.einsum('hts,bcshd->bcthd', w, vn) + b_s.T.astype(vn.dtype)[None, None, :, :, None]
    return u * sv.reshape(B, S, GM_WIDTH)


def stick_breaking_attention(q, k, v):
    B, S, H, D = q.shape
    scale = D ** -0.5
    outs = []
    for i in range(S // SB_BLOCK):
        q0 = i * SB_BLOCK
        L = q0 + SB_BLOCK
        qb = q[:, q0:L]
        kp = k[:, :L]
        vp = v[:, :L]
        z = jnp.einsum('bthd,bshd->bhts', qb, kp).astype(jnp.float32) * scale
        t_idx = q0 + jnp.arange(SB_BLOCK)[:, None]
        s_idx = jnp.arange(L)[None, :]
        causal = s_idx < t_idx
        log_1m = jnp.where(causal, -jax.nn.softplus(z), 0.0)
        after = lax.cumsum(log_1m, axis=3, reverse=True) - log_1m
        a = jnp.where(causal, jnp.exp(jax.nn.log_sigmoid(z) + after), 0.0)
        outs.append(jnp.einsum('bhts,bshd->bthd', a.astype(vp.dtype), vp))
    return jnp.concatenate(outs, axis=1)


def setup_inputs(seed: int = 0) -> dict:
    key = jax.random.key(seed)
    ks = jax.random.split(key, 18)
    f32 = jnp.float32

    def nrm(k, shape, fan_in):
        return jax.random.normal(k, shape, f32) * (fan_in ** -0.5)

    def gain(k, shape):
        return jnp.ones(shape, f32) + 0.05 * jax.random.normal(k, shape, f32)

    return {
        "x": jax.random.normal(ks[0], (BATCH, SEQ, D_MODEL), f32),
        "p": jax.random.normal(ks[1], (DEPTH, BATCH, SEQ, PLE_DIM), f32),
        "ffn1_norm": gain(ks[2], (DEPTH, D_MODEL)),
        "ffn1_w_in": nrm(ks[3], (DEPTH, D_MODEL, 2 * D_FF), D_MODEL),
        "ffn1_w_out": nrm(ks[4], (DEPTH, D_FF, D_MODEL), D_FF),
        "mix_norm": gain(ks[5], (DEPTH, D_MODEL)),
        "w_mix_in": nrm(ks[6], (DEPTH, D_MODEL, MIX_IN_WIDTH), D_MODEL),
        "gmlp_v_norm": gain(ks[7], (DEPTH, GM_WIDTH)),
        "gmlp_w_s": nrm(ks[8], (DEPTH, GM_HEADS, CHUNK, CHUNK), CHUNK),
        "gmlp_b": jnp.ones((DEPTH, GM_HEADS, CHUNK), f32) + 0.1 * jax.random.normal(ks[9], (DEPTH, GM_HEADS, CHUNK), f32),
        "w_mix_out": nrm(ks[10], (DEPTH, D_MIX, D_MODEL), D_MIX),
        "ffn2_norm": gain(ks[11], (DEPTH, D_MODEL)),
        "ffn2_w_in": nrm(ks[12], (DEPTH, D_MODEL, 2 * D_FF), D_MODEL),
        "ffn2_w_out": nrm(ks[13], (DEPTH, D_FF, D_MODEL), D_FF),
        "ple_norm": gain(ks[14], (DEPTH, D_MODEL)),
        "ple_w_gate": nrm(ks[15], (DEPTH, D_MODEL, D_MODEL), D_MODEL),
        "ple_w_proj": nrm(ks[16], (DEPTH, PLE_DIM, D_MODEL), PLE_DIM),
        "final_norm": gain(ks[17], (D_MODEL,)),
    }


def reference(x, p, ffn1_norm, ffn1_w_in, ffn1_w_out, mix_norm, w_mix_in, gmlp_v_norm,
              gmlp_w_s, gmlp_b, w_mix_out, ffn2_norm, ffn2_w_in, ffn2_w_out,
              ple_norm, ple_w_gate, ple_w_proj, final_norm):
    B, S, _ = x.shape
    splits = [GM_WIDTH, 2 * GM_WIDTH, 2 * GM_WIDTH + SB_WIDTH, 2 * GM_WIDTH + 2 * SB_WIDTH]
    h = x
    for i in range(DEPTH):
        h = h + 0.5 * swiglu(rms_norm(h, ffn1_norm[i]), ffn1_w_in[i], ffn1_w_out[i])

        n = rms_norm(h, mix_norm[i])
        zmix = n @ w_mix_in[i]
        gm_u, gm_v, sb_q, sb_k, sb_v = jnp.split(zmix, splits, axis=-1)
        gm_out = chunked_gmlp(jax.nn.gelu(gm_u, approximate=False),
                              jax.nn.gelu(gm_v, approximate=False),
                              gmlp_v_norm[i], gmlp_w_s[i], gmlp_b[i])
        sb_out = stick_breaking_attention(sb_q.reshape(B, S, SB_HEADS, SB_HEAD_DIM),
                                          sb_k.reshape(B, S, SB_HEADS, SB_HEAD_DIM),
                                          sb_v.reshape(B, S, SB_HEADS, SB_HEAD_DIM))
        mixed = jnp.concatenate([gm_out, sb_out.reshape(B, S, SB_WIDTH)], axis=-1)
        h = h + mixed @ w_mix_out[i]

        h = h + 0.5 * swiglu(rms_norm(h, ffn2_norm[i]), ffn2_w_in[i], ffn2_w_out[i])

        gate = jax.nn.sigmoid(rms_norm(h, ple_norm[i]) @ ple_w_gate[i])
        h = h + gate * (p[i] @ ple_w_proj[i])
    return rms_norm(h, final_norm)
```

```python
import functools

import jax
import jax.numpy as jnp
from jax import lax
from jax.experimental import pallas as pl
from jax.experimental.pallas import tpu as pltpu

D_MODEL = 1024
PLE_DIM = 256
D_FF = 2816
CHUNK = 128
GM_HEADS = 4
GM_HEAD_DIM = 128
GM_WIDTH = GM_HEADS * GM_HEAD_DIM
SB_HEADS = 8
SB_HEAD_DIM = 64
SB_WIDTH = SB_HEADS * SB_HEAD_DIM
EPS = 1e-6

LANES = 128
FF_CHUNK = 256
TOKEN_TILE = 512
ATTN_TILE = 128
HEADS_PER_GROUP = LANES // SB_HEAD_DIM
VMEM_LIMIT = 56 * 1024 * 1024

BF16 = jnp.bfloat16
F32 = jnp.float32


def _dot(a, b):
    return jnp.dot(a, b, preferred_element_type=F32)


def _rms(x, g):
    return x * lax.rsqrt(jnp.mean(x * x, axis=-1, keepdims=True) + EPS) * g


def _half_swiglu(y, w_in_ref, w_out_ref, act_ref):
    for c in range(D_FF // FF_CHUNK):
        lo = c * FF_CHUNK
        gate = _dot(y, w_in_ref[:, lo:lo + FF_CHUNK])
        up = _dot(y, w_in_ref[:, D_FF + lo:D_FF + lo + FF_CHUNK])
        act_ref[:, lo:lo + FF_CHUNK] = (gate * jax.nn.sigmoid(gate) * up).astype(BF16)
    return _dot(act_ref[...], w_out_ref[...])


def _gelu(x):
    return 0.5 * x * (1.0 + lax.erf(x * (2.0 ** -0.5)))


def _pre_kernel(x_ref, n1_ref, w_in_ref, w_out_ref, nm_ref, w_mix_ref, vg_ref,
                ws_ref, bs_ref, h_ref, gm_ref, q_ref, k_ref, v_ref, act_ref):
    x = x_ref[...]
    y = _rms(x, n1_ref[...]).astype(BF16)
    h = x + 0.5 * _half_swiglu(y, w_in_ref, w_out_ref, act_ref)
    h_ref[...] = h

    n = _rms(h, nm_ref[...]).astype(BF16)
    u = _gelu(_dot(n, w_mix_ref[:, 0:GM_WIDTH]))
    v = _gelu(_dot(n, w_mix_ref[:, GM_WIDTH:2 * GM_WIDTH]))
    vn = _rms(v, vg_ref[...]).astype(BF16)

    t_idx = lax.broadcasted_iota(jnp.int32, (CHUNK, CHUNK), 0)
    s_idx = lax.broadcasted_iota(jnp.int32, (CHUNK, CHUNK), 1)
    keep = s_idx <= t_idx
    for hd in range(GM_HEADS):
        w = jnp.where(keep, ws_ref[hd], 0.0).astype(BF16)
        cols = slice(hd * GM_HEAD_DIM, (hd + 1) * GM_HEAD_DIM)
        for c in range(x.shape[0] // CHUNK):
            rows = slice(c * CHUNK, (c + 1) * CHUNK)
            sv = _dot(w, vn[rows, cols]) + bs_ref[:, cols]
            gm_ref[rows, cols] = (u[rows, cols] * sv).astype(BF16)

    base = 2 * GM_WIDTH
    q_ref[...] = (_dot(n, w_mix_ref[:, base:base + SB_WIDTH]) * (SB_HEAD_DIM ** -0.5)).astype(BF16)
    k_ref[...] = _dot(n, w_mix_ref[:, base + SB_WIDTH:base + 2 * SB_WIDTH]).astype(BF16)
    v_ref[...] = _dot(n, w_mix_ref[:, base + 2 * SB_WIDTH:base + 3 * SB_WIDTH]).astype(BF16)


def _resident(shape):
    return pl.BlockSpec(shape, lambda *_: (0,) * len(shape), pipeline_mode=pl.Buffered(1))


def _pre_call(x2, n1, w_in, w_out, nm, w_mix, vg, ws, bs_full):
    tokens = x2.shape[0]
    tm = TOKEN_TILE
    row = lambda width: pl.BlockSpec((tm, width), lambda i: (i, 0))
    return pl.pallas_call(
        _pre_kernel,
        grid=(tokens // tm,),
        in_specs=[
            row(D_MODEL),
            _resident((1, D_MODEL)),
            _resident(w_in.shape),
            _resident(w_out.shape),
            _resident((1, D_MODEL)),
            _resident(w_mix.shape),
            _resident((1, GM_WIDTH)),
            _resident(ws.shape),
            _resident(bs_full.shape),
        ],
        out_specs=[row(D_MODEL), row(GM_WIDTH), row(SB_WIDTH), row(SB_WIDTH), row(SB_WIDTH)],
        out_shape=[
            jax.ShapeDtypeStruct((tokens, D_MODEL), F32),
            jax.ShapeDtypeStruct((tokens, GM_WIDTH), BF16),
            jax.ShapeDtypeStruct((tokens, SB_WIDTH), BF16),
            jax.ShapeDtypeStruct((tokens, SB_WIDTH), BF16),
            jax.ShapeDtypeStruct((tokens, SB_WIDTH), BF16),
        ],
        scratch_shapes=[pltpu.VMEM((tm, D_FF), BF16)],
        compiler_params=pltpu.CompilerParams(
            dimension_semantics=("arbitrary",), vmem_limit_bytes=VMEM_LIMIT),
        name="pre_ffn_mix_in",
    )(x2, n1, w_in, w_out, nm, w_mix, vg, ws, bs_full)


def _attn_kernel(q_ref, k_ref, v_ref, o_ref):
    t = ATTN_TILE
    qi = pl.program_id(2)
    q2 = q_ref[0]
    lane = lax.broadcasted_iota(jnp.int32, (t, LANES), 1)
    head_q = [jnp.where((lane // SB_HEAD_DIM) == hh, q2, jnp.zeros_like(q2))
              for hh in range(HEADS_PER_GROUP)]

    j_idx = lax.broadcasted_iota(jnp.int32, (2 * t, t), 0) % t
    s_idx = lax.broadcasted_iota(jnp.int32, (2 * t, t), 1)
    later = (j_idx > s_idx).astype(BF16)

    row = lax.broadcasted_iota(jnp.int32, (t, t), 0)
    col = lax.broadcasted_iota(jnp.int32, (t, t), 1)
    strict = col < row

    def block(j, state, diagonal):
        start = pl.multiple_of(j * t, t)
        kb = k_ref[0, pl.ds(start, t), :]
        vb = v_ref[0, pl.ds(start, t), :]
        new_state = []
        for hh in range(HEADS_PER_GROUP):
            acc, carry = state[hh]
            z = lax.dot_general(head_q[hh], kb, (((1,), (1,)), ((), ())),
                                preferred_element_type=F32)
            sp = jnp.maximum(z, 0.0) + jnp.log(1.0 + jnp.exp(-jnp.abs(z)))
            log1m = -sp
            if diagonal:
                log1m = jnp.where(strict, log1m, 0.0)
            hi = log1m.astype(BF16)
            lo = (log1m - hi.astype(F32)).astype(BF16)
            after = _dot(jnp.concatenate([hi, lo], axis=1), later) + carry
            a = jnp.exp((z - sp) + after)
            if diagonal:
                a = jnp.where(strict, a, 0.0)
            acc = acc + _dot(a.astype(BF16), vb)
            carry = carry + jnp.sum(log1m, axis=1, keepdims=True)
            new_state.append((acc, carry))
        return tuple(new_state)

    init = tuple((jnp.zeros((t, LANES), F32), jnp.zeros((t, 1), F32))
                 for _ in range(HEADS_PER_GROUP))
    state = block(qi, init, True)
    state = lax.fori_loop(0, qi, lambda i, s: block(qi - 1 - i, s, False), state)

    out = state[0][0]
    for hh in range(1, HEADS_PER_GROUP):
        out = jnp.where((lane // SB_HEAD_DIM) == hh, state[hh][0], out)
    o_ref[0] = out.astype(o_ref.dtype)


def _attn_call(q, k, v):
    batch, seq, _ = q.shape
    t = ATTN_TILE
    groups = SB_WIDTH // LANES
    q_spec = pl.BlockSpec((1, t, LANES), lambda b, g, i: (b, i, g))
    kv_spec = pl.BlockSpec((1, seq, LANES), lambda b, g, i: (b, 0, g))
    return pl.pallas_call(
        _attn_kernel,
        grid=(batch, groups, seq // t),
        in_specs=[q_spec, kv_spec, kv_spec],
        out_specs=q_spec,
        out_shape=jax.ShapeDtypeStruct((batch, seq, SB_WIDTH), BF16),
        compiler_params=pltpu.CompilerParams(
            dimension_semantics=("arbitrary", "arbitrary", "arbitrary"),
            vmem_limit_bytes=VMEM_LIMIT),
        name="stick_breaking_attn",
    )(q, k, v)


def _post_kernel(h_ref, gm_ref, sb_ref, p_ref, w_mo_ref, n2_ref, w_in_ref, w_out_ref,
                 np_ref, w_gate_ref, w_proj_ref, nf_ref, o_ref, act_ref, *, final):
    mixed = jnp.concatenate([gm_ref[...], sb_ref[...]], axis=1)
    h = h_ref[...] + _dot(mixed, w_mo_ref[...])
    y = _rms(h, n2_ref[...]).astype(BF16)
    h = h + 0.5 * _half_swiglu(y, w_in_ref, w_out_ref, act_ref)
    gate = jax.nn.sigmoid(_dot(_rms(h, np_ref[...]).astype(BF16), w_gate_ref[...]))
    h = h + gate * _dot(p_ref[...].astype(BF16), w_proj_ref[...])
    o_ref[...] = _rms(h, nf_ref[...]) if final else h


def _post_call(h1, gm, sb, p2, w_mo, n2, w_in, w_out, np_, w_gate, w_proj, nf, final):
    tokens = h1.shape[0]
    tm = TOKEN_TILE
    row = lambda width: pl.BlockSpec((tm, width), lambda i: (i, 0))
    return pl.pallas_call(
        functools.partial(_post_kernel, final=final),
        grid=(tokens // tm,),
        in_specs=[
            row(D_MODEL), row(GM_WIDTH), row(SB_WIDTH), row(PLE_DIM),
            _resident(w_mo.shape),
            _resident((1, D_MODEL)),
            _resident(w_in.shape),
            _resident(w_out.shape),
            _resident((1, D_MODEL)),
            _resident(w_gate.shape),
            _resident(w_proj.shape),
            _resident((1, D_MODEL)),
        ],
        out_specs=row(D_MODEL),
        out_shape=jax.ShapeDtypeStruct((tokens, D_MODEL), F32),
        scratch_shapes=[pltpu.VMEM((tm, D_FF), BF16)],
        compiler_params=pltpu.CompilerParams(
            dimension_semantics=("arbitrary",), vmem_limit_bytes=VMEM_LIMIT),
        name="post_mix_out_ffn_ple",
    )(h1, gm, sb, p2, w_mo, n2, w_in, w_out, np_, w_gate, w_proj, nf)


def kernel(x, p, ffn1_norm, ffn1_w_in, ffn1_w_out, mix_norm, w_mix_in, gmlp_v_norm, gmlp_w_s, gmlp_b, w_mix_out, ffn2_norm, ffn2_w_in, ffn2_w_out, ple_norm, ple_w_gate, ple_w_proj, final_norm):
    batch, seq, _ = x.shape
    depth = p.shape[0]
    tokens = batch * seq
    vec = lambda g: g.reshape(1, -1).astype(F32)

    h = x.reshape(tokens, D_MODEL)
    for i in range(depth):
        bs_full = jnp.repeat(gmlp_b[i].T, GM_HEAD_DIM, axis=1)
        h1, gm, q, k, v = _pre_call(
            h, vec(ffn1_norm[i]), ffn1_w_in[i].astype(BF16), ffn1_w_out[i].astype(BF16),
            vec(mix_norm[i]), w_mix_in[i].astype(BF16), vec(gmlp_v_norm[i]),
            gmlp_w_s[i], bs_full)
        shape3 = (batch, seq, SB_WIDTH)
        sb = _attn_call(q.reshape(shape3), k.reshape(shape3), v.reshape(shape3))
        h = _post_call(
            h1, gm, sb.reshape(tokens, SB_WIDTH), p[i].reshape(tokens, PLE_DIM),
            w_mix_out[i].astype(BF16), vec(ffn2_norm[i]), ffn2_w_in[i].astype(BF16),
            ffn2_w_out[i].astype(BF16), vec(ple_norm[i]), ple_w_gate[i].astype(BF16),
            ple_w_proj[i].astype(BF16),
            vec(final_norm), final=(i == depth - 1))
    return h.reshape(batch, seq, D_MODEL)
```

```python
import functools

import jax
import jax.numpy as jnp
from jax import lax
from jax.experimental import pallas as pl
from jax.experimental.pallas import tpu as pltpu

D_MODEL = 1024
PLE_DIM = 256
D_FF = 2816
CHUNK = 128
GM_HEADS = 4
GM_HEAD_DIM = 128
GM_WIDTH = GM_HEADS * GM_HEAD_DIM
SB_HEADS = 8
SB_HEAD_DIM = 64
SB_WIDTH = SB_HEADS * SB_HEAD_DIM
EPS = 1e-6

LANES = 128
FF_CHUNK = 256
TOKEN_TILE = 512
ATTN_TILE = 512
ATTN_SUB = 128
LOG_UNDERFLOW = -88.0
VMEM_LIMIT = 56 * 1024 * 1024

BF16 = jnp.bfloat16
F32 = jnp.float32


def _dot(a, b):
    return jnp.dot(a, b, preferred_element_type=F32)


def _rms(x, g):
    return x * lax.rsqrt(jnp.mean(x * x, axis=-1, keepdims=True) + EPS) * g


def _half_swiglu(y, w_in_ref, w_out_ref, act_ref):
    for c in range(D_FF // FF_CHUNK):
        lo = c * FF_CHUNK
        gate = _dot(y, w_in_ref[:, lo:lo + FF_CHUNK])
        up = _dot(y, w_in_ref[:, D_FF + lo:D_FF + lo + FF_CHUNK])
        act_ref[:, lo:lo + FF_CHUNK] = (gate * jax.nn.sigmoid(gate) * up).astype(BF16)
    return _dot(act_ref[...], w_out_ref[...])


def _gelu(x):
    return 0.5 * x * (1.0 + lax.erf(x * (2.0 ** -0.5)))


def _pre_kernel(x_ref, n1_ref, w_in_ref, w_out_ref, nm_ref, w_mix_ref, vg_ref,
                ws_ref, bs_ref, h_ref, gm_ref, q_ref, k_ref, v_ref, act_ref):
    x = x_ref[...]
    y = _rms(x, n1_ref[...]).astype(BF16)
    h = x + 0.5 * _half_swiglu(y, w_in_ref, w_out_ref, act_ref)
    h_ref[...] = h

    n = _rms(h, nm_ref[...]).astype(BF16)
    u = _gelu(_dot(n, w_mix_ref[:, 0:GM_WIDTH]))
    v = _gelu(_dot(n, w_mix_ref[:, GM_WIDTH:2 * GM_WIDTH]))
    vn = _rms(v, vg_ref[...]).astype(BF16)

    t_idx = lax.broadcasted_iota(jnp.int32, (CHUNK, CHUNK), 0)
    s_idx = lax.broadcasted_iota(jnp.int32, (CHUNK, CHUNK), 1)
    keep = s_idx <= t_idx
    for hd in range(GM_HEADS):
        w = jnp.where(keep, ws_ref[hd], 0.0).astype(BF16)
        cols = slice(hd * GM_HEAD_DIM, (hd + 1) * GM_HEAD_DIM)
        for c in range(x.shape[0] // CHUNK):
            rows = slice(c * CHUNK, (c + 1) * CHUNK)
            sv = _dot(w, vn[rows, cols]) + bs_ref[:, cols]
            gm_ref[rows, cols] = (u[rows, cols] * sv).astype(BF16)

    base = 2 * GM_WIDTH
    q_ref[...] = (_dot(n, w_mix_ref[:, base:base + SB_WIDTH]) * (SB_HEAD_DIM ** -0.5)).astype(BF16)
    k_ref[...] = _dot(n, w_mix_ref[:, base + SB_WIDTH:base + 2 * SB_WIDTH]).astype(BF16)
    v_ref[...] = _dot(n, w_mix_ref[:, base + 2 * SB_WIDTH:base + 3 * SB_WIDTH]).astype(BF16)


def _resident(shape):
    return pl.BlockSpec(shape, lambda *_: (0,) * len(shape), pipeline_mode=pl.Buffered(1))


def _pre_call(x2, n1, w_in, w_out, nm, w_mix, vg, ws, bs_full):
    tokens = x2.shape[0]
    tm = TOKEN_TILE
    row = lambda width: pl.BlockSpec((tm, width), lambda i: (i, 0))
    return pl.pallas_call(
        _pre_kernel,
        grid=(tokens // tm,),
        in_specs=[
            row(D_MODEL),
            _resident((1, D_MODEL)),
            _resident(w_in.shape),
            _resident(w_out.shape),
            _resident((1, D_MODEL)),
            _resident(w_mix.shape),
            _resident((1, GM_WIDTH)),
            _resident(ws.shape),
            _resident(bs_full.shape),
        ],
        out_specs=[row(D_MODEL), row(GM_WIDTH), row(SB_WIDTH), row(SB_WIDTH), row(SB_WIDTH)],
        out_shape=[
            jax.ShapeDtypeStruct((tokens, D_MODEL), F32),
            jax.ShapeDtypeStruct((tokens, GM_WIDTH), BF16),
            jax.ShapeDtypeStruct((tokens, SB_WIDTH), BF16),
            jax.ShapeDtypeStruct((tokens, SB_WIDTH), BF16),
            jax.ShapeDtypeStruct((tokens, SB_WIDTH), BF16),
        ],
        scratch_shapes=[pltpu.VMEM((tm, D_FF), BF16)],
        compiler_params=pltpu.CompilerParams(
            dimension_semantics=("arbitrary",), vmem_limit_bytes=VMEM_LIMIT),
        name="pre_ffn_mix_in",
    )(x2, n1, w_in, w_out, nm, w_mix, vg, ws, bs_full)


def _later_matrix(n):
    j = lax.broadcasted_iota(jnp.int32, (2 * n, n), 0) % n
    s = lax.broadcasted_iota(jnp.int32, (2 * n, n), 1)
    return (j > s).astype(BF16)


def _stack_heads_q(q2, first):
    zero = jnp.zeros_like(q2)
    return jnp.concatenate([jnp.where(first, q2, zero), jnp.where(first, zero, q2)], axis=0)


def _block_weights(qs, kb, vb, later, mask, carry):
    n = kb.shape[0]
    z = lax.dot_general(qs, kb, (((1,), (1,)), ((), ())), preferred_element_type=F32)
    sp = jnp.maximum(z, 0.0) + jnp.log(1.0 + jnp.exp(-jnp.abs(z)))
    log1m = -sp
    if mask is not None:
        log1m = jnp.where(mask, log1m, 0.0)
    hi = log1m.astype(BF16)
    lo = (log1m - hi.astype(F32)).astype(BF16)
    after = _dot(jnp.concatenate([hi, lo], axis=1), later)
    if carry is not None:
        after = after + carry
    a = jnp.exp((z - sp) + after)
    if mask is not None:
        a = jnp.where(mask, a, 0.0)
    ab = a.astype(BF16)
    first = lax.broadcasted_iota(jnp.int32, (n, LANES), 1) < SB_HEAD_DIM
    zero = jnp.zeros_like(vb)
    v_stack = jnp.concatenate([jnp.where(first, vb, zero), jnp.where(first, zero, vb)], axis=0)
    pv = _dot(jnp.concatenate([ab[:ATTN_SUB], ab[ATTN_SUB:]], axis=1), v_stack)
    return pv, jnp.sum(log1m, axis=1, keepdims=True)


def _attn_kernel(q_ref, k_ref, v_ref, o_ref, acc_ref, carry_ref):
    sub, win = ATTN_SUB, 2 * ATTN_SUB
    n_sub = q_ref.shape[1] // sub
    qi = pl.program_id(2)
    first = lax.broadcasted_iota(jnp.int32, (sub, LANES), 1) < SB_HEAD_DIM
    later_win = _later_matrix(win)
    row = lax.broadcasted_iota(jnp.int32, (2 * sub, win), 0) % sub
    col = lax.broadcasted_iota(jnp.int32, (2 * sub, win), 1)

    need = []
    for s in range(n_sub):
        tile = qi * n_sub + s
        rows = pl.ds(s * sub, sub)
        start = pl.multiple_of(jnp.maximum(tile - 1, 0) * sub, sub)
        offset = tile * sub - start
        mask = col < row + offset
        qs = _stack_heads_q(q_ref[0, rows, :], first)
        kb = k_ref[0, pl.ds(start, win), :]
        vb = v_ref[0, pl.ds(start, win), :]
        pv, rowsum = _block_weights(qs, kb, vb, later_win, mask, None)
        acc_ref[rows, :] = pv
        carry_ref[s] = rowsum
        need.append(jnp.logical_and(jnp.max(rowsum) > LOG_UNDERFLOW, tile >= 2))

    later_sub = _later_matrix(sub)
    for s in range(n_sub):
        tile = qi * n_sub + s
        rows = pl.ds(s * sub, sub)

        @pl.when(need[s])
        def _():
            qs = _stack_heads_q(q_ref[0, rows, :], first)

            def cond(st):
                return jnp.logical_and(st[0] >= 0, st[1])

            def body(st):
                j, _, acc, carry = st
                start = pl.multiple_of(j * sub, sub)
                kb = k_ref[0, pl.ds(start, sub), :]
                vb = v_ref[0, pl.ds(start, sub), :]
                pv, rowsum = _block_weights(qs, kb, vb, later_sub, None, carry)
                carry = carry + rowsum
                return j - 1, jnp.max(carry) > LOG_UNDERFLOW, acc + pv, carry

            st = lax.while_loop(cond, body, (tile - 2, True, acc_ref[rows, :], carry_ref[s]))
            acc_ref[rows, :] = st[2]

    o_ref[0] = acc_ref[...].astype(o_ref.dtype)


def _attn_call(q, k, v):
    batch, seq, _ = q.shape
    tq = ATTN_TILE
    groups = SB_WIDTH // LANES
    q_spec = pl.BlockSpec((1, tq, LANES), lambda b, g, i: (b, i, g))
    kv_spec = pl.BlockSpec((1, seq, LANES), lambda b, g, i: (b, 0, g))
    return pl.pallas_call(
        _attn_kernel,
        grid=(batch, groups, seq // tq),
        in_specs=[q_spec, kv_spec, kv_spec],
        out_specs=q_spec,
        out_shape=jax.ShapeDtypeStruct((batch, seq, SB_WIDTH), BF16),
        scratch_shapes=[pltpu.VMEM((tq, LANES), F32),
                        pltpu.VMEM((tq // ATTN_SUB, 2 * ATTN_SUB, 1), F32)],
        compiler_params=pltpu.CompilerParams(
            dimension_semantics=("arbitrary", "arbitrary", "arbitrary"),
            vmem_limit_bytes=VMEM_LIMIT),
        name="stick_breaking_attn",
    )(q, k, v)


def _post_kernel(h_ref, gm_ref, sb_ref, p_ref, w_mo_ref, n2_ref, w_in_ref, w_out_ref,
                 np_ref, w_gate_ref, w_proj_ref, nf_ref, o_ref, act_ref, *, final):
    mixed = jnp.concatenate([gm_ref[...], sb_ref[...]], axis=1)
    h = h_ref[...] + _dot(mixed, w_mo_ref[...])
    y = _rms(h, n2_ref[...]).astype(BF16)
    h = h + 0.5 * _half_swiglu(y, w_in_ref, w_out_ref, act_ref)
    gate = jax.nn.sigmoid(_dot(_rms(h, np_ref[...]).astype(BF16), w_gate_ref[...]))
    h = h + gate * _dot(p_ref[...].astype(BF16), w_proj_ref[...])
    o_ref[...] = _rms(h, nf_ref[...]) if final else h


def _post_call(h1, gm, sb, p2, w_mo, n2, w_in, w_out, np_, w_gate, w_proj, nf, final):
    tokens = h1.shape[0]
    tm = TOKEN_TILE
    row = lambda width: pl.BlockSpec((tm, width), lambda i: (i, 0))
    return pl.pallas_call(
        functools.partial(_post_kernel, final=final),
        grid=(tokens // tm,),
        in_specs=[
            row(D_MODEL), row(GM_WIDTH), row(SB_WIDTH), row(PLE_DIM),
            _resident(w_mo.shape),
            _resident((1, D_MODEL)),
            _resident(w_in.shape),
            _resident(w_out.shape),
            _resident((1, D_MODEL)),
            _resident(w_gate.shape),
            _resident(w_proj.shape),
            _resident((1, D_MODEL)),
        ],
        out_specs=row(D_MODEL),
        out_shape=jax.ShapeDtypeStruct((tokens, D_MODEL), F32),
        scratch_shapes=[pltpu.VMEM((tm, D_FF), BF16)],
        compiler_params=pltpu.CompilerParams(
            dimension_semantics=("arbitrary",), vmem_limit_bytes=VMEM_LIMIT),
        name="post_mix_out_ffn_ple",
    )(h1, gm, sb, p2, w_mo, n2, w_in, w_out, np_, w_gate, w_proj, nf)


def kernel(x, p, ffn1_norm, ffn1_w_in, ffn1_w_out, mix_norm, w_mix_in, gmlp_v_norm, gmlp_w_s, gmlp_b, w_mix_out, ffn2_norm, ffn2_w_in, ffn2_w_out, ple_norm, ple_w_gate, ple_w_proj, final_norm):
    batch, seq, _ = x.shape
    depth = p.shape[0]
    tokens = batch * seq
    vec = lambda g: g.reshape(1, -1).astype(F32)

    h = x.reshape(tokens, D_MODEL)
    for i in range(depth):
        bs_full = jnp.repeat(gmlp_b[i].T, GM_HEAD_DIM, axis=1)
        h1, gm, q, k, v = _pre_call(
            h, vec(ffn1_norm[i]), ffn1_w_in[i].astype(BF16), ffn1_w_out[i].astype(BF16),
            vec(mix_norm[i]), w_mix_in[i].astype(BF16), vec(gmlp_v_norm[i]),
            gmlp_w_s[i], bs_full)
        shape3 = (batch, seq, SB_WIDTH)
        sb = _attn_call(q.reshape(shape3), k.reshape(shape3), v.reshape(shape3))
        h = _post_call(
            h1, gm, sb.reshape(tokens, SB_WIDTH), p[i].reshape(tokens, PLE_DIM),
            w_mix_out[i].astype(BF16), vec(ffn2_norm[i]), ffn2_w_in[i].astype(BF16),
            ffn2_w_out[i].astype(BF16), vec(ple_norm[i]), ple_w_gate[i].astype(BF16),
            ple_w_proj[i].astype(BF16),
            vec(final_norm), final=(i == depth - 1))
    return h.reshape(batch, seq, D_MODEL)
```

```python
import functools

import jax
import jax.numpy as jnp
from jax import lax
from jax.experimental import pallas as pl
from jax.experimental.pallas import tpu as pltpu

D_MODEL = 1024
PLE_DIM = 256
D_FF = 2816
CHUNK = 128
GM_HEADS = 4
GM_HEAD_DIM = 128
GM_WIDTH = GM_HEADS * GM_HEAD_DIM
SB_HEADS = 8
SB_HEAD_DIM = 64
SB_WIDTH = SB_HEADS * SB_HEAD_DIM
EPS = 1e-6

LANES = 128
FF_CHUNK = 256
TOKEN_TILE = 512
ATTN_TILE = 1024
ATTN_SUB = 128
LOG2E = 1.4426950408889634
DECAY_LIMIT = 127.0
VMEM_LIMIT = 56 * 1024 * 1024

BF16 = jnp.bfloat16
F32 = jnp.float32


def _dot(a, b):
    return jnp.dot(a, b, preferred_element_type=F32)


def _rms(x, g):
    return x * lax.rsqrt(jnp.mean(x * x, axis=-1, keepdims=True) + EPS) * g


def _half_swiglu(y, w_in_ref, w_out_ref, act_ref):
    for c in range(D_FF // FF_CHUNK):
        lo = c * FF_CHUNK
        gate = _dot(y, w_in_ref[:, lo:lo + FF_CHUNK])
        up = _dot(y, w_in_ref[:, D_FF + lo:D_FF + lo + FF_CHUNK])
        act_ref[:, lo:lo + FF_CHUNK] = (gate * jax.nn.sigmoid(gate) * up).astype(BF16)
    return _dot(act_ref[...], w_out_ref[...])


def _gelu(x):
    return 0.5 * x * (1.0 + lax.erf(x * (2.0 ** -0.5)))


def _pre_kernel(x_ref, n1_ref, w_in_ref, w_out_ref, nm_ref, w_mix_ref, vg_ref,
                ws_ref, bs_ref, h_ref, gm_ref, q_ref, k_ref, v_ref, act_ref):
    x = x_ref[...]
    y = _rms(x, n1_ref[...]).astype(BF16)
    h = x + 0.5 * _half_swiglu(y, w_in_ref, w_out_ref, act_ref)
    h_ref[...] = h

    n = _rms(h, nm_ref[...]).astype(BF16)
    u = _gelu(_dot(n, w_mix_ref[:, 0:GM_WIDTH]))
    v = _gelu(_dot(n, w_mix_ref[:, GM_WIDTH:2 * GM_WIDTH]))
    vn = _rms(v, vg_ref[...]).astype(BF16)

    t_idx = lax.broadcasted_iota(jnp.int32, (CHUNK, CHUNK), 0)
    s_idx = lax.broadcasted_iota(jnp.int32, (CHUNK, CHUNK), 1)
    keep = s_idx <= t_idx
    for hd in range(GM_HEADS):
        w = jnp.where(keep, ws_ref[hd], 0.0).astype(BF16)
        cols = slice(hd * GM_HEAD_DIM, (hd + 1) * GM_HEAD_DIM)
        for c in range(x.shape[0] // CHUNK):
            rows = slice(c * CHUNK, (c + 1) * CHUNK)
            sv = _dot(w, vn[rows, cols]) + bs_ref[:, cols]
            gm_ref[rows, cols] = (u[rows, cols] * sv).astype(BF16)

    base = 2 * GM_WIDTH
    q_ref[...] = (_dot(n, w_mix_ref[:, base:base + SB_WIDTH])
                  * (SB_HEAD_DIM ** -0.5 * LOG2E)).astype(BF16)
    k_ref[...] = _dot(n, w_mix_ref[:, base + SB_WIDTH:base + 2 * SB_WIDTH]).astype(BF16)
    v_ref[...] = _dot(n, w_mix_ref[:, base + 2 * SB_WIDTH:base + 3 * SB_WIDTH]).astype(BF16)


def _resident(shape):
    return pl.BlockSpec(shape, lambda *_: (0,) * len(shape), pipeline_mode=pl.Buffered(1))


def _pre_call(x2, n1, w_in, w_out, nm, w_mix, vg, ws, bs_full):
    tokens = x2.shape[0]
    tm = TOKEN_TILE
    row = lambda width: pl.BlockSpec((tm, width), lambda i: (i, 0))
    return pl.pallas_call(
        _pre_kernel,
        grid=(tokens // tm,),
        in_specs=[
            row(D_MODEL),
            _resident((1, D_MODEL)),
            _resident(w_in.shape),
            _resident(w_out.shape),
            _resident((1, D_MODEL)),
            _resident(w_mix.shape),
            _resident((1, GM_WIDTH)),
            _resident(ws.shape),
            _resident(bs_full.shape),
        ],
        out_specs=[row(D_MODEL), row(GM_WIDTH), row(SB_WIDTH), row(SB_WIDTH), row(SB_WIDTH)],
        out_shape=[
            jax.ShapeDtypeStruct((tokens, D_MODEL), F32),
            jax.ShapeDtypeStruct((tokens, GM_WIDTH), BF16),
            jax.ShapeDtypeStruct((tokens, SB_WIDTH), BF16),
            jax.ShapeDtypeStruct((tokens, SB_WIDTH), BF16),
            jax.ShapeDtypeStruct((tokens, SB_WIDTH), BF16),
        ],
        scratch_shapes=[pltpu.VMEM((tm, D_FF), BF16)],
        compiler_params=pltpu.CompilerParams(
            dimension_semantics=("arbitrary",), vmem_limit_bytes=VMEM_LIMIT),
        name="pre_ffn_mix_in",
    )(x2, n1, w_in, w_out, nm, w_mix, vg, ws, bs_full)


def _later_matrix(n):
    j = lax.broadcasted_iota(jnp.int32, (2 * n, n), 0) % n
    s = lax.broadcasted_iota(jnp.int32, (2 * n, n), 1)
    return (j > s).astype(BF16)


def _stack_heads(x):
    first = lax.broadcasted_iota(jnp.int32, x.shape, 1) < SB_HEAD_DIM
    zero = jnp.zeros_like(x)
    return jnp.concatenate([jnp.where(first, x, zero), jnp.where(first, zero, x)], axis=0)


def _scores(qs, kb):
    return lax.dot_general(qs, kb, (((1,), (1,)), ((), ())), preferred_element_type=F32)


def _softplus2(z):
    return jnp.maximum(z, 0.0) + jnp.log2(1.0 + jnp.exp2(-jnp.abs(z)))


def _mask_window(x, diff, off):
    left, right = x[:, :ATTN_SUB], x[:, ATTN_SUB:]
    if not (isinstance(off, int) and off == ATTN_SUB):
        left = jnp.where(diff < off, left, 0.0)
    right = jnp.where(diff < off - ATTN_SUB, right, 0.0)
    return jnp.concatenate([left, right], axis=1)


def _later_sum(sp, later):
    hi = sp.astype(BF16)
    lo = (sp - hi.astype(F32)).astype(BF16)
    return _dot(jnp.concatenate([hi, lo], axis=1), later)


def _weights(z, sp, decay):
    return jnp.exp2((z - sp) - decay)


def _apply(a, vb):
    pv = _dot(a.astype(BF16), vb)
    first = lax.broadcasted_iota(jnp.int32, (ATTN_SUB, LANES), 1) < SB_HEAD_DIM
    return jnp.where(first, pv[:ATTN_SUB], pv[ATTN_SUB:])


def _attn_kernel(q_ref, k_ref, v_ref, o_ref, acc_ref, decay_ref):
    sub, win = ATTN_SUB, 2 * ATTN_SUB
    n_sub = q_ref.shape[1] // sub
    qi = pl.program_id(2)
    later_win = _later_matrix(win)
    diff = (lax.broadcasted_iota(jnp.int32, (2 * sub, sub), 1)
            - lax.broadcasted_iota(jnp.int32, (2 * sub, sub), 0) % sub)
    subs = range(n_sub)
    tile = [qi * n_sub + s for s in subs]
    rows = [pl.ds(s * sub, sub) for s in subs]

    start = [pl.multiple_of(jnp.maximum(tile[0] - 1, 0) * sub, sub)] + [
        pl.multiple_of(tile[s] * sub - sub, sub) for s in subs[1:]]
    off = [tile[0] * sub - start[0]] + [sub] * (n_sub - 1)
    z = [_scores(_stack_heads(q_ref[0, rows[s], :]), k_ref[0, pl.ds(start[s], win), :])
         for s in subs]
    sp = [_softplus2(z[s]) for s in subs]
    spm = [_mask_window(sp[s], diff, off[s]) for s in subs]
    later = [_later_sum(spm[s], later_win) for s in subs]
    a = [_mask_window(_weights(z[s], sp[s], later[s]), diff, off[s]) for s in subs]
    pv = [_apply(a[s], v_ref[0, pl.ds(start[s], win), :]) for s in subs]
    decay = [jnp.sum(spm[s], axis=1, keepdims=True) for s in subs]
    need = []
    for s in subs:
        acc_ref[rows[s], :] = pv[s]
        decay_ref[s] = decay[s]
        need.append(jnp.logical_and(jnp.min(decay[s]) < DECAY_LIMIT, tile[s] >= 2))

    later_sub = _later_matrix(sub)
    for s in subs:

        @pl.when(need[s])
        def _():
            qs = _stack_heads(q_ref[0, rows[s], :])

            def cond(st):
                return jnp.logical_and(st[0] >= 0, st[1])

            def body(st):
                j, _, acc, decay = st
                first_key = pl.multiple_of(j * sub, sub)
                z = _scores(qs, k_ref[0, pl.ds(first_key, sub), :])
                sp = _softplus2(z)
                a = _weights(z, sp, _later_sum(sp, later_sub) + decay)
                acc = acc + _apply(a, v_ref[0, pl.ds(first_key, sub), :])
                decay = decay + jnp.sum(sp, axis=1, keepdims=True)
                return j - 1, jnp.min(decay) < DECAY_LIMIT, acc, decay

            st = lax.while_loop(cond, body, (tile[s] - 2, True, acc_ref[rows[s], :], decay_ref[s]))
            acc_ref[rows[s], :] = st[2]

    o_ref[0] = acc_ref[...].astype(o_ref.dtype)


def _attn_call(q, k, v):
    batch, seq, _ = q.shape
    tq = ATTN_TILE
    groups = SB_WIDTH // LANES
    q_spec = pl.BlockSpec((1, tq, LANES), lambda b, g, i: (b, i, g))
    kv_spec = pl.BlockSpec((1, seq, LANES), lambda b, g, i: (b, 0, g))
    return pl.pallas_call(
        _attn_kernel,
        grid=(batch, groups, seq // tq),
        in_specs=[q_spec, kv_spec, kv_spec],
        out_specs=q_spec,
        out_shape=jax.ShapeDtypeStruct((batch, seq, SB_WIDTH), BF16),
        scratch_shapes=[pltpu.VMEM((tq, LANES), F32),
                        pltpu.VMEM((tq // ATTN_SUB, 2 * ATTN_SUB, 1), F32)],
        compiler_params=pltpu.CompilerParams(
            dimension_semantics=("arbitrary", "arbitrary", "arbitrary"),
            vmem_limit_bytes=VMEM_LIMIT),
        name="stick_breaking_attn",
    )(q, k, v)


def _post_kernel(h_ref, gm_ref, sb_ref, p_ref, w_mo_ref, n2_ref, w_in_ref, w_out_ref,
                 np_ref, w_gate_ref, w_proj_ref, nf_ref, o_ref, act_ref, *, final):
    mixed = jnp.concatenate([gm_ref[...], sb_ref[...]], axis=1)
    h = h_ref[...] + _dot(mixed, w_mo_ref[...])
    y = _rms(h, n2_ref[...]).astype(BF16)
    h = h + 0.5 * _half_swiglu(y, w_in_ref, w_out_ref, act_ref)
    gate = jax.nn.sigmoid(_dot(_rms(h, np_ref[...]).astype(BF16), w_gate_ref[...]))
    h = h + gate * _dot(p_ref[...].astype(BF16), w_proj_ref[...])
    o_ref[...] = _rms(h, nf_ref[...]) if final else h


def _post_call(h1, gm, sb, p2, w_mo, n2, w_in, w_out, np_, w_gate, w_proj, nf, final):
    tokens = h1.shape[0]
    tm = TOKEN_TILE
    row = lambda width: pl.BlockSpec((tm, width), lambda i: (i, 0))
    return pl.pallas_call(
        functools.partial(_post_kernel, final=final),
        grid=(tokens // tm,),
        in_specs=[
            row(D_MODEL), row(GM_WIDTH), row(SB_WIDTH), row(PLE_DIM),
            _resident(w_mo.shape),
            _resident((1, D_MODEL)),
            _resident(w_in.shape),
            _resident(w_out.shape),
            _resident((1, D_MODEL)),
            _resident(w_gate.shape),
            _resident(w_proj.shape),
            _resident((1, D_MODEL)),
        ],
        out_specs=row(D_MODEL),
        out_shape=jax.ShapeDtypeStruct((tokens, D_MODEL), F32),
        scratch_shapes=[pltpu.VMEM((tm, D_FF), BF16)],
        compiler_params=pltpu.CompilerParams(
            dimension_semantics=("arbitrary",), vmem_limit_bytes=VMEM_LIMIT),
        name="post_mix_out_ffn_ple",
    )(h1, gm, sb, p2, w_mo, n2, w_in, w_out, np_, w_gate, w_proj, nf)


def kernel(x, p, ffn1_norm, ffn1_w_in, ffn1_w_out, mix_norm, w_mix_in, gmlp_v_norm, gmlp_w_s, gmlp_b, w_mix_out, ffn2_norm, ffn2_w_in, ffn2_w_out, ple_norm, ple_w_gate, ple_w_proj, final_norm):
    batch, seq, _ = x.shape
    depth = p.shape[0]
    tokens = batch * seq
    vec = lambda g: g.reshape(1, -1).astype(F32)

    h = x.reshape(tokens, D_MODEL)
    for i in range(depth):
        bs_full = jnp.repeat(gmlp_b[i].T, GM_HEAD_DIM, axis=1)
        h1, gm, q, k, v = _pre_call(
            h, vec(ffn1_norm[i]), ffn1_w_in[i].astype(BF16), ffn1_w_out[i].astype(BF16),
            vec(mix_norm[i]), w_mix_in[i].astype(BF16), vec(gmlp_v_norm[i]),
            gmlp_w_s[i], bs_full)
        shape3 = (batch, seq, SB_WIDTH)
        sb = _attn_call(q.reshape(shape3), k.reshape(shape3), v.reshape(shape3))
        h = _post_call(
            h1, gm, sb.reshape(tokens, SB_WIDTH), p[i].reshape(tokens, PLE_DIM),
            w_mix_out[i].astype(BF16), vec(ffn2_norm[i]), ffn2_w_in[i].astype(BF16),
            ffn2_w_out[i].astype(BF16), vec(ple_norm[i]), ple_w_gate[i].astype(BF16),
            ple_w_proj[i].astype(BF16), vec(final_norm), final=(i == depth - 1))
    return h.reshape(batch, seq, D_MODEL)
```

```python
import functools

import jax
import jax.numpy as jnp
from jax import lax
from jax.experimental import pallas as pl
from jax.experimental.pallas import tpu as pltpu

D_MODEL = 1024
PLE_DIM = 256
D_FF = 2816
CHUNK = 128
GM_HEADS = 4
GM_HEAD_DIM = 128
GM_WIDTH = GM_HEADS * GM_HEAD_DIM
SB_HEADS = 8
SB_HEAD_DIM = 64
SB_WIDTH = SB_HEADS * SB_HEAD_DIM
EPS = 1e-6

LANES = 128
FF_CHUNK = 256
TOKEN_TILE = 512
ATTN_SUB = 128
ATTN_GROUPS = 8
LOG2E = 1.4426950408889634
DECAY_LIMIT = 127.0
VMEM_LIMIT = 56 * 1024 * 1024

BF16 = jnp.bfloat16
F32 = jnp.float32


def _dot(a, b):
    return jnp.dot(a, b, preferred_element_type=F32)


def _rms(x, g):
    return x * lax.rsqrt(jnp.mean(x * x, axis=-1, keepdims=True) + EPS) * g


def _half_swiglu(y, w_in_ref, w_out_ref, act_ref, side_work=()):
    for c in range(D_FF // FF_CHUNK):
        if c < len(side_work):
            side_work[c]()
        lo = c * FF_CHUNK
        gate = _dot(y, w_in_ref[:, lo:lo + FF_CHUNK])
        up = _dot(y, w_in_ref[:, D_FF + lo:D_FF + lo + FF_CHUNK])
        act_ref[:, lo:lo + FF_CHUNK] = (gate * jax.nn.sigmoid(gate) * up).astype(BF16)
    return _dot(act_ref[...], w_out_ref[...])


def _gelu(x):
    return 0.5 * x * (1.0 + lax.erf(x * (2.0 ** -0.5)))


def _pre_kernel(x_ref, n1_ref, w_in_ref, w_out_ref, nm_ref, w_mix_ref, vg_ref,
                ws_ref, bs_ref, h_ref, gm_ref, q_ref, k_ref, v_ref, act_ref):
    x = x_ref[...]
    y = _rms(x, n1_ref[...]).astype(BF16)
    h = x + 0.5 * _half_swiglu(y, w_in_ref, w_out_ref, act_ref)
    h_ref[...] = h

    n = _rms(h, nm_ref[...]).astype(BF16)
    u = _gelu(_dot(n, w_mix_ref[:, 0:GM_WIDTH]))
    v = _gelu(_dot(n, w_mix_ref[:, GM_WIDTH:2 * GM_WIDTH]))
    vn = _rms(v, vg_ref[...]).astype(BF16)

    t_idx = lax.broadcasted_iota(jnp.int32, (CHUNK, CHUNK), 0)
    s_idx = lax.broadcasted_iota(jnp.int32, (CHUNK, CHUNK), 1)
    keep = s_idx <= t_idx
    for hd in range(GM_HEADS):
        w = jnp.where(keep, ws_ref[hd], 0.0).astype(BF16)
        cols = slice(hd * GM_HEAD_DIM, (hd + 1) * GM_HEAD_DIM)
        for c in range(x.shape[0] // CHUNK):
            rows = slice(c * CHUNK, (c + 1) * CHUNK)
            sv = _dot(w, vn[rows, cols]) + bs_ref[:, cols]
            gm_ref[rows, cols] = (u[rows, cols] * sv).astype(BF16)

    base = 2 * GM_WIDTH
    q_ref[...] = (_dot(n, w_mix_ref[:, base:base + SB_WIDTH])
                  * (SB_HEAD_DIM ** -0.5 * LOG2E)).astype(BF16)
    k_ref[...] = _dot(n, w_mix_ref[:, base + SB_WIDTH:base + 2 * SB_WIDTH]).astype(BF16)
    v_ref[...] = _dot(n, w_mix_ref[:, base + 2 * SB_WIDTH:base + 3 * SB_WIDTH]).astype(BF16)


def _resident(shape):
    return pl.BlockSpec(shape, lambda *_: (0,) * len(shape), pipeline_mode=pl.Buffered(1))


def _pre_call(x2, n1, w_in, w_out, nm, w_mix, vg, ws, bs_full):
    tokens = x2.shape[0]
    tm = TOKEN_TILE
    row = lambda width: pl.BlockSpec((tm, width), lambda i: (i, 0))
    return pl.pallas_call(
        _pre_kernel,
        grid=(tokens // tm,),
        in_specs=[
            row(D_MODEL),
            _resident((1, D_MODEL)),
            _resident(w_in.shape),
            _resident(w_out.shape),
            _resident((1, D_MODEL)),
            _resident(w_mix.shape),
            _resident((1, GM_WIDTH)),
            _resident(ws.shape),
            _resident(bs_full.shape),
        ],
        out_specs=[row(D_MODEL), row(GM_WIDTH), row(SB_WIDTH), row(SB_WIDTH), row(SB_WIDTH)],
        out_shape=[
            jax.ShapeDtypeStruct((tokens, D_MODEL), F32),
            jax.ShapeDtypeStruct((tokens, GM_WIDTH), BF16),
            jax.ShapeDtypeStruct((tokens, SB_WIDTH), BF16),
            jax.ShapeDtypeStruct((tokens, SB_WIDTH), BF16),
            jax.ShapeDtypeStruct((tokens, SB_WIDTH), BF16),
        ],
        scratch_shapes=[pltpu.VMEM((tm, D_FF), BF16)],
        compiler_params=pltpu.CompilerParams(
            dimension_semantics=("arbitrary",), vmem_limit_bytes=VMEM_LIMIT),
        name="pre_ffn_mix_in",
    )(x2, n1, w_in, w_out, nm, w_mix, vg, ws, bs_full)


def _later_matrix(n):
    j = lax.broadcasted_iota(jnp.int32, (2 * n, n), 0) % n
    s = lax.broadcasted_iota(jnp.int32, (2 * n, n), 1)
    return (j > s).astype(BF16)


def _stack_heads(x):
    first = lax.broadcasted_iota(jnp.int32, x.shape, 1) < SB_HEAD_DIM
    zero = jnp.zeros_like(x)
    return jnp.concatenate([jnp.where(first, x, zero), jnp.where(first, zero, x)], axis=0)


def _scores(qs, kb):
    return lax.dot_general(qs, kb, (((1,), (1,)), ((), ())), preferred_element_type=F32)


def _softplus2(z):
    return jnp.maximum(z, 0.0) + jnp.log2(1.0 + jnp.exp2(-jnp.abs(z)))


def _mask_window(x, diff, left_valid):
    left, right = x[:, :ATTN_SUB], x[:, ATTN_SUB:]
    if left_valid is not True:
        left = jnp.where(left_valid, left, 0.0)
    right = jnp.where(diff < 0, right, 0.0)
    return jnp.concatenate([left, right], axis=1)


def _split(sp):
    hi = sp.astype(BF16)
    lo = (sp - hi.astype(F32)).astype(BF16)
    return jnp.concatenate([hi, lo], axis=1)


def _later_sum(sp, later):
    return _dot(_split(sp), later)


def _weights(z, sp, decay):
    return jnp.exp2((z - sp) - decay)


def _apply(a, vb):
    pv = _dot(a.astype(BF16), vb)
    first = lax.broadcasted_iota(jnp.int32, (ATTN_SUB, LANES), 1) < SB_HEAD_DIM
    return jnp.where(first, pv[:ATTN_SUB], pv[ATTN_SUB:])


def _window_consts():
    sub = ATTN_SUB
    diff = (lax.broadcasted_iota(jnp.int32, (2 * sub, sub), 1)
            - lax.broadcasted_iota(jnp.int32, (2 * sub, sub), 0) % sub)
    return _later_matrix(2 * sub), diff


def _attention_windows(q_ref, kp_ref, ko_ref, vp_ref, vo_ref, out_ref, need_ref, qi, n_groups):
    sub = ATTN_SUB
    n_sub = q_ref.shape[1] // sub
    all_units = [(g, s) for g in range(SB_WIDTH // LANES) for s in range(n_sub)]
    per_group = len(all_units) // n_groups
    lanes = lambda g: slice(g * LANES, (g + 1) * LANES)

    def window(prev_ref, own_ref, g, s):
        if s == 0:
            return jnp.concatenate([prev_ref[0, :, lanes(g)], own_ref[0, 0:sub, lanes(g)]], axis=0)
        return own_ref[0, (s - 1) * sub:(s + 1) * sub, lanes(g)]

    later_win, diff = _window_consts()
    groups = [all_units[k * per_group:(k + 1) * per_group] for k in range(n_groups)]
    left_valid = [[qi > 0 if s == 0 else True for _, s in units] for units in groups]
    state = [{} for _ in groups]

    def stage_scores(k):
        units, st = groups[k], state[k]
        st["z"] = [_scores(_stack_heads(q_ref[0, s * sub:(s + 1) * sub, lanes(g)]),
                           window(kp_ref, ko_ref, g, s)) for g, s in units]
        st["sp"] = [_softplus2(x) for x in st["z"]]
        spm = [_mask_window(x, diff, lv) for x, lv in zip(st["sp"], left_valid[k])]
        st["split"] = [_split(x) for x in spm]
        for u, (g, s) in enumerate(units):
            decay = jnp.sum(spm[u], axis=1, keepdims=True)
            unfinished = jnp.logical_and(jnp.min(decay) < DECAY_LIMIT, qi * n_sub + s >= 2)
            need_ref[k * per_group + u] = unfinished.astype(jnp.int32)

    def stage_later(k):
        st = state[k]
        later = [_dot(x, later_win) for x in st["split"]]
        st["a"] = [_mask_window(_weights(*zsl), diff, lv).astype(BF16)
                   for zsl, lv in zip(zip(st["z"], st["sp"], later), left_valid[k])]

    def stage_values(k):
        for a, (g, s) in zip(state[k]["a"], groups[k]):
            pv = _apply(a, window(vp_ref, vo_ref, g, s))
            out_ref[s * sub:(s + 1) * sub, lanes(g)] = pv.astype(out_ref.dtype)

    def thunk(t):
        for stage, k in ((stage_values, t - 2), (stage_later, t - 1), (stage_scores, t)):
            if 0 <= k < n_groups:
                stage(k)

    return [functools.partial(thunk, t) for t in range(n_groups + 2)]


def _attention_sweeps(q_ref, k_any, v_any, out_ref, need_ref, kbuf, vbuf, sems, b, qi):
    sub = ATTN_SUB
    n_sub = q_ref.shape[1] // sub
    later_win, diff = _window_consts()
    later_sub = _later_matrix(sub)

    for g in range(SB_WIDTH // LANES):
        lanes = slice(g * LANES, (g + 1) * LANES)

        def fetch(first_key, n):
            copies = [pltpu.make_async_copy(src.at[b, pl.ds(first_key, n), lanes],
                                            dst.at[pl.ds(0, n)], sems.at[i])
                      for i, (src, dst) in enumerate(((k_any, kbuf), (v_any, vbuf)))]
            for c in copies:
                c.start()
            for c in copies:
                c.wait()

        def per_sub(s, carry):
            @pl.when(need_ref[g * n_sub + s] != 0)
            def _():
                tile = qi * n_sub + s
                rows = pl.ds(pl.multiple_of(s * sub, sub), sub)
                qs = _stack_heads(q_ref[0, rows, lanes])
                fetch(pl.multiple_of((tile - 1) * sub, sub), 2 * sub)
                z = _scores(qs, kbuf[...])
                sp = _softplus2(z)
                spm = _mask_window(sp, diff, True)
                a = _mask_window(_weights(z, sp, _later_sum(spm, later_win)), diff, True)
                acc = _apply(a, vbuf[...])
                decay = jnp.sum(spm, axis=1, keepdims=True)

                def cond(st):
                    return jnp.logical_and(st[0] >= 0, st[1])

                def body(st):
                    j, _, acc, decay = st
                    fetch(pl.multiple_of(j * sub, sub), sub)
                    z = _scores(qs, kbuf[0:sub, :])
                    sp = _softplus2(z)
                    a = _weights(z, sp, _later_sum(sp, later_sub) + decay)
                    acc = acc + _apply(a, vbuf[0:sub, :])
                    decay = decay + jnp.sum(sp, axis=1, keepdims=True)
                    return j - 1, jnp.min(decay) < DECAY_LIMIT, acc, decay

                st = lax.while_loop(cond, body, (tile - 2, True, acc, decay))
                out_ref[rows, lanes] = st[2].astype(out_ref.dtype)

            return carry

        lax.fori_loop(0, n_sub, per_sub, 0)


def _post_kernel(h_ref, gm_ref, p_ref, q_ref, kp_ref, ko_ref, vp_ref, vo_ref, k_any, v_any,
                 w_mo_ref, n2_ref, w_in_ref, w_out_ref, np_ref, w_gate_ref, w_proj_ref, nf_ref,
                 o_ref, act_ref, sb_ref, need_ref, kbuf, vbuf, sems, *, final, tiles_per_seq):
    j = pl.program_id(0)
    n_tiles = pl.num_programs(0) - 1
    cur = j % 2

    @pl.when(j == 0)
    def _():
        sb_ref[1] = jnp.zeros(sb_ref.shape[1:], sb_ref.dtype)

    tile = jnp.minimum(j, n_tiles - 1)
    b, qi = tile // tiles_per_seq, tile % tiles_per_seq
    attention = _attention_windows(q_ref, kp_ref, ko_ref, vp_ref, vo_ref, sb_ref.at[cur],
                                   need_ref, qi, ATTN_GROUPS)

    mixed = jnp.concatenate([gm_ref[...], sb_ref[1 - cur]], axis=1)
    h = h_ref[...] + _dot(mixed, w_mo_ref[...])
    y = _rms(h, n2_ref[...]).astype(BF16)
    h = h + 0.5 * _half_swiglu(y, w_in_ref, w_out_ref, act_ref, attention)
    gate = jax.nn.sigmoid(_dot(_rms(h, np_ref[...]).astype(BF16), w_gate_ref[...]))
    h = h + gate * _dot(p_ref[...].astype(BF16), w_proj_ref[...])
    o_ref[...] = _rms(h, nf_ref[...]) if final else h

    _attention_sweeps(q_ref, k_any, v_any, sb_ref.at[cur], need_ref, kbuf, vbuf, sems, b, qi)


def _post_call(h1, gm, p2, q, k, v, w_mo, n2, w_in, w_out, np_, w_gate, w_proj, nf, final):
    tokens = h1.shape[0]
    batch, seq, _ = q.shape
    tm = TOKEN_TILE
    n_tiles = tokens // tm
    tiles_per_seq = seq // tm
    sub_per_tile = tm // ATTN_SUB
    dense = lambda width: pl.BlockSpec((tm, width), lambda j: (jnp.maximum(j - 1, 0), 0))

    def own(j):
        t = jnp.minimum(j, n_tiles - 1)
        return t // tiles_per_seq, t % tiles_per_seq, 0

    def prev(j):
        b, qi, _ = own(j)
        return b, jnp.maximum(qi * sub_per_tile - 1, 0), 0

    own_spec = pl.BlockSpec((1, tm, SB_WIDTH), own)
    prev_spec = pl.BlockSpec((1, ATTN_SUB, SB_WIDTH), prev)
    hbm = pl.BlockSpec(memory_space=pl.ANY)
    return pl.pallas_call(
        functools.partial(_post_kernel, final=final, tiles_per_seq=tiles_per_seq),
        grid=(n_tiles + 1,),
        in_specs=[
            dense(D_MODEL), dense(GM_WIDTH), dense(PLE_DIM),
            own_spec, prev_spec, own_spec, prev_spec, own_spec, hbm, hbm,
            _resident(w_mo.shape),
            _resident((1, D_MODEL)),
            _resident(w_in.shape),
            _resident(w_out.shape),
            _resident((1, D_MODEL)),
            _resident(w_gate.shape),
            _resident(w_proj.shape),
            _resident((1, D_MODEL)),
        ],
        out_specs=dense(D_MODEL),
        out_shape=jax.ShapeDtypeStruct((tokens, D_MODEL), F32),
        scratch_shapes=[
            pltpu.VMEM((tm, D_FF), BF16),
            pltpu.VMEM((2, tm, SB_WIDTH), BF16),
            pltpu.SMEM((SB_WIDTH // LANES * sub_per_tile,), jnp.int32),
            pltpu.VMEM((2 * ATTN_SUB, LANES), BF16),
            pltpu.VMEM((2 * ATTN_SUB, LANES), BF16),
            pltpu.SemaphoreType.DMA((2,)),
        ],
        compiler_params=pltpu.CompilerParams(
            dimension_semantics=("arbitrary",), vmem_limit_bytes=VMEM_LIMIT),
        name="post_attn_mix_out_ffn_ple",
    )(h1, gm, p2, q, k, k, v, v, k, v, w_mo, n2, w_in, w_out, np_, w_gate, w_proj, nf)


def kernel(x, p, ffn1_norm, ffn1_w_in, ffn1_w_out, mix_norm, w_mix_in, gmlp_v_norm, gmlp_w_s, gmlp_b, w_mix_out, ffn2_norm, ffn2_w_in, ffn2_w_out, ple_norm, ple_w_gate, ple_w_proj, final_norm):
    batch, seq, _ = x.shape
    depth = p.shape[0]
    tokens = batch * seq
    vec = lambda g: g.reshape(1, -1).astype(F32)

    h = x.reshape(tokens, D_MODEL)
    for i in range(depth):
        bs_full = jnp.repeat(gmlp_b[i].T, GM_HEAD_DIM, axis=1)
        h1, gm, q, k, v = _pre_call(
            h, vec(ffn1_norm[i]), ffn1_w_in[i].astype(BF16), ffn1_w_out[i].astype(BF16),
            vec(mix_norm[i]), w_mix_in[i].astype(BF16), vec(gmlp_v_norm[i]),
            gmlp_w_s[i], bs_full)
        shape3 = (batch, seq, SB_WIDTH)
        h = _post_call(
            h1, gm, p[i].reshape(tokens, PLE_DIM),
            q.reshape(shape3), k.reshape(shape3), v.reshape(shape3),
            w_mix_out[i].astype(BF16), vec(ffn2_norm[i]), ffn2_w_in[i].astype(BF16),
            ffn2_w_out[i].astype(BF16), vec(ple_norm[i]), ple_w_gate[i].astype(BF16),
            ple_w_proj[i].astype(BF16), vec(final_norm), final=(i == depth - 1))
    return h.reshape(batch, seq, D_MODEL)
```

```python
import functools

import jax
import jax.numpy as jnp
from jax import lax
from jax.experimental import pallas as pl
from jax.experimental.pallas import tpu as pltpu

D_MODEL = 1024
PLE_DIM = 256
D_FF = 2816
CHUNK = 128
GM_HEADS = 4
GM_HEAD_DIM = 128
GM_WIDTH = GM_HEADS * GM_HEAD_DIM
SB_HEADS = 8
SB_HEAD_DIM = 64
SB_WIDTH = SB_HEADS * SB_HEAD_DIM
EPS = 1e-6

LANES = 128
FF_CHUNK = 256
TOKEN_TILE = 512
ATTN_SUB = 128
ATTN_GROUPS = 8
LOG2E = 1.4426950408889634
DECAY_LIMIT = 127.0
VMEM_LIMIT = 56 * 1024 * 1024

BF16 = jnp.bfloat16
F32 = jnp.float32


def _dot(a, b):
    return jnp.dot(a, b, preferred_element_type=F32)


def _rms(x, g):
    return x * lax.rsqrt(jnp.mean(x * x, axis=-1, keepdims=True) + EPS) * g


def _half_swiglu(y, w_in_ref, w_out_ref, act_ref, side_work=()):
    for c in range(D_FF // FF_CHUNK):
        if c < len(side_work):
            side_work[c]()
        lo = c * FF_CHUNK
        gate = _dot(y, w_in_ref[:, lo:lo + FF_CHUNK])
        up = _dot(y, w_in_ref[:, D_FF + lo:D_FF + lo + FF_CHUNK])
        act_ref[:, lo:lo + FF_CHUNK] = (gate * jax.nn.sigmoid(gate) * up).astype(BF16)
    return _dot(act_ref[...], w_out_ref[...])


def _gelu(x):
    return 0.5 * x * (1.0 + lax.erf(x * (2.0 ** -0.5)))


def _pre_kernel(x_ref, n1_ref, w_in_ref, w_out_ref, nm_ref, w_mix_ref, vg_ref,
                ws_ref, bs_ref, h_ref, gm_ref, q_ref, k_ref, v_ref, act_ref):
    x = x_ref[...]
    y = _rms(x, n1_ref[...]).astype(BF16)
    h = x + 0.5 * _half_swiglu(y, w_in_ref, w_out_ref, act_ref)
    h_ref[...] = h

    n = _rms(h, nm_ref[...]).astype(BF16)
    zu = _dot(n, w_mix_ref[:, 0:GM_WIDTH])
    zv = _dot(n, w_mix_ref[:, GM_WIDTH:2 * GM_WIDTH])

    base = 2 * GM_WIDTH
    q_ref[...] = (_dot(n, w_mix_ref[:, base:base + SB_WIDTH])
                  * (SB_HEAD_DIM ** -0.5 * LOG2E)).astype(BF16)
    k_ref[...] = _dot(n, w_mix_ref[:, base + SB_WIDTH:base + 2 * SB_WIDTH]).astype(BF16)
    v_ref[...] = _dot(n, w_mix_ref[:, base + 2 * SB_WIDTH:base + 3 * SB_WIDTH]).astype(BF16)

    u = _gelu(zu)
    v = _gelu(zv)
    vn = _rms(v, vg_ref[...]).astype(BF16)

    t_idx = lax.broadcasted_iota(jnp.int32, (CHUNK, CHUNK), 0)
    s_idx = lax.broadcasted_iota(jnp.int32, (CHUNK, CHUNK), 1)
    keep = s_idx <= t_idx
    for hd in range(GM_HEADS):
        w = jnp.where(keep, ws_ref[hd], 0.0).astype(BF16)
        cols = slice(hd * GM_HEAD_DIM, (hd + 1) * GM_HEAD_DIM)
        for c in range(x.shape[0] // CHUNK):
            rows = slice(c * CHUNK, (c + 1) * CHUNK)
            sv = _dot(w, vn[rows, cols]) + bs_ref[:, cols]
            gm_ref[rows, cols] = (u[rows, cols] * sv).astype(BF16)


def _resident(shape):
    return pl.BlockSpec(shape, lambda *_: (0,) * len(shape), pipeline_mode=pl.Buffered(1))


def _pre_call(x2, n1, w_in, w_out, nm, w_mix, vg, ws, bs_full):
    tokens = x2.shape[0]
    tm = TOKEN_TILE
    row = lambda width: pl.BlockSpec((tm, width), lambda i: (i, 0))
    return pl.pallas_call(
        _pre_kernel,
        grid=(tokens // tm,),
        in_specs=[
            row(D_MODEL),
            _resident((1, D_MODEL)),
            _resident(w_in.shape),
            _resident(w_out.shape),
            _resident((1, D_MODEL)),
            _resident(w_mix.shape),
            _resident((1, GM_WIDTH)),
            _resident(ws.shape),
            _resident(bs_full.shape),
        ],
        out_specs=[row(D_MODEL), row(GM_WIDTH), row(SB_WIDTH), row(SB_WIDTH), row(SB_WIDTH)],
        out_shape=[
            jax.ShapeDtypeStruct((tokens, D_MODEL), F32),
            jax.ShapeDtypeStruct((tokens, GM_WIDTH), BF16),
            jax.ShapeDtypeStruct((tokens, SB_WIDTH), BF16),
            jax.ShapeDtypeStruct((tokens, SB_WIDTH), BF16),
            jax.ShapeDtypeStruct((tokens, SB_WIDTH), BF16),
        ],
        scratch_shapes=[pltpu.VMEM((tm, D_FF), BF16)],
        compiler_params=pltpu.CompilerParams(
            dimension_semantics=("arbitrary",), vmem_limit_bytes=VMEM_LIMIT),
        name="pre_ffn_mix_in",
    )(x2, n1, w_in, w_out, nm, w_mix, vg, ws, bs_full)


def _later_matrix(n):
    j = lax.broadcasted_iota(jnp.int32, (2 * n, n), 0) % n
    s = lax.broadcasted_iota(jnp.int32, (2 * n, n), 1)
    return (j > s).astype(BF16)


def _stack_heads(x):
    first = lax.broadcasted_iota(jnp.int32, x.shape, 1) < SB_HEAD_DIM
    zero = jnp.zeros_like(x)
    return jnp.concatenate([jnp.where(first, x, zero), jnp.where(first, zero, x)], axis=0)


def _scores(qs, kb):
    return lax.dot_general(qs, kb, (((1,), (1,)), ((), ())), preferred_element_type=F32)


def _softplus2(z):
    return jnp.maximum(z, 0.0) + jnp.log2(1.0 + jnp.exp2(-jnp.abs(z)))


def _mask_window(x, diff, left_valid):
    left, right = x[:, :ATTN_SUB], x[:, ATTN_SUB:]
    if left_valid is not True:
        left = jnp.where(left_valid, left, 0.0)
    right = jnp.where(diff < 0, right, 0.0)
    return jnp.concatenate([left, right], axis=1)


def _split(sp):
    hi = sp.astype(BF16)
    lo = (sp - hi.astype(F32)).astype(BF16)
    return jnp.concatenate([hi, lo], axis=1)


def _later_sum(sp, later):
    return _dot(_split(sp), later)


def _weights(z, sp, decay):
    return jnp.exp2((z - sp) - decay)


def _apply(a, vb):
    pv = _dot(a.astype(BF16), vb)
    first = lax.broadcasted_iota(jnp.int32, (ATTN_SUB, LANES), 1) < SB_HEAD_DIM
    return jnp.where(first, pv[:ATTN_SUB], pv[ATTN_SUB:])


def _window_consts():
    sub = ATTN_SUB
    diff = (lax.broadcasted_iota(jnp.int32, (2 * sub, sub), 1)
            - lax.broadcasted_iota(jnp.int32, (2 * sub, sub), 0) % sub)
    return _later_matrix(2 * sub), diff


def _attention_windows(q_ref, kp_ref, ko_ref, vp_ref, vo_ref, out_ref, acc_ref, decay_ref,
                       need_ref, qi, n_groups):
    sub = ATTN_SUB
    n_sub = q_ref.shape[1] // sub
    all_units = [(g, s) for g in range(SB_WIDTH // LANES) for s in range(n_sub)]
    per_group = len(all_units) // n_groups
    lanes = lambda g: slice(g * LANES, (g + 1) * LANES)

    def window(prev_ref, own_ref, g, s):
        if s == 0:
            return jnp.concatenate([prev_ref[0, sub:2 * sub, lanes(g)],
                                    own_ref[0, 0:sub, lanes(g)]], axis=0)
        return own_ref[0, (s - 1) * sub:(s + 1) * sub, lanes(g)]

    later_win, diff = _window_consts()
    groups = [all_units[k * per_group:(k + 1) * per_group] for k in range(n_groups)]
    left_valid = [[qi > 0 if s == 0 else True for _, s in units] for units in groups]
    state = [{} for _ in groups]

    def stage_scores(k):
        units, st = groups[k], state[k]
        st["z"] = [_scores(_stack_heads(q_ref[0, s * sub:(s + 1) * sub, lanes(g)]),
                           window(kp_ref, ko_ref, g, s)) for g, s in units]
        st["sp"] = [_softplus2(x) for x in st["z"]]
        spm = [_mask_window(x, diff, lv) for x, lv in zip(st["sp"], left_valid[k])]
        st["split"] = [_split(x) for x in spm]
        for u, (g, s) in enumerate(units):
            decay = jnp.sum(spm[u], axis=1, keepdims=True)
            decay_ref[s, :, g:g + 1] = decay
            unfinished = jnp.logical_and(jnp.min(decay) < DECAY_LIMIT, qi * n_sub + s >= 2)
            need_ref[k * per_group + u] = unfinished.astype(jnp.int32)

    def stage_later(k):
        st = state[k]
        later = [_dot(x, later_win) for x in st["split"]]
        st["a"] = [_mask_window(_weights(*zsl), diff, lv).astype(BF16)
                   for zsl, lv in zip(zip(st["z"], st["sp"], later), left_valid[k])]

    def stage_values(k):
        for a, (g, s) in zip(state[k]["a"], groups[k]):
            pv = _apply(a, window(vp_ref, vo_ref, g, s))
            acc_ref[s * sub:(s + 1) * sub, lanes(g)] = pv
            out_ref[s * sub:(s + 1) * sub, lanes(g)] = pv.astype(out_ref.dtype)

    def thunk(t):
        for stage, k in ((stage_values, t - 2), (stage_later, t - 1), (stage_scores, t)):
            if 0 <= k < n_groups:
                stage(k)

    return [functools.partial(thunk, t) for t in range(n_groups + 2)]


def _attention_sweeps(q_ref, kp_ref, ko_ref, vp_ref, vo_ref, k_any, v_any, out_ref, acc_ref,
                      decay_ref, need_ref, kbuf, vbuf, sems, b, qi):
    sub = ATTN_SUB
    n_sub = q_ref.shape[1] // sub
    later_sub = _later_matrix(sub)

    for g in range(SB_WIDTH // LANES):
        lanes = slice(g * LANES, (g + 1) * LANES)

        def fetch(first_key):
            copies = [pltpu.make_async_copy(src.at[b, pl.ds(first_key, sub), lanes], dst, sems.at[i])
                      for i, (src, dst) in enumerate(((k_any, kbuf), (v_any, vbuf)))]
            for c in copies:
                c.start()
            for c in copies:
                c.wait()

        def per_sub(s, carry):
            @pl.when(need_ref[g * n_sub + s] != 0)
            def _():
                tile = qi * n_sub + s
                rows = pl.ds(pl.multiple_of(s * sub, sub), sub)
                qs = _stack_heads(q_ref[0, rows, lanes])

                def block(kb, vb, acc, decay):
                    z = _scores(qs, kb)
                    sp = _softplus2(z)
                    a = _weights(z, sp, _later_sum(sp, later_sub) + decay)
                    return acc + _apply(a, vb), decay + jnp.sum(sp, axis=1, keepdims=True)

                def resident(prev_ref, own_ref):
                    in_own = pl.ds(pl.multiple_of(jnp.maximum(s - 2, 0) * sub, sub), sub)
                    in_prev = pl.ds(pl.multiple_of(jnp.minimum(s, 1) * sub, sub), sub)
                    return lax.cond(s >= 2, lambda: own_ref[0, in_own, lanes],
                                    lambda: prev_ref[0, in_prev, lanes])

                acc, decay = block(resident(kp_ref, ko_ref), resident(vp_ref, vo_ref),
                                   acc_ref[rows, lanes], decay_ref[s, :, g:g + 1])

                def cond(st):
                    return jnp.logical_and(st[0] >= 0, st[1])

                def body(st):
                    j, _, acc, decay = st
                    fetch(pl.multiple_of(j * sub, sub))
                    acc, decay = block(kbuf[...], vbuf[...], acc, decay)
                    return j - 1, jnp.min(decay) < DECAY_LIMIT, acc, decay

                st = lax.while_loop(cond, body,
                                    (tile - 3, jnp.min(decay) < DECAY_LIMIT, acc, decay))
                out_ref[rows, lanes] = st[2].astype(out_ref.dtype)

            return carry

        lax.fori_loop(0, n_sub, per_sub, 0)


def _post_kernel(h_ref, gm_ref, p_ref, q_ref, kp_ref, ko_ref, vp_ref, vo_ref, k_any, v_any,
                 w_mo_ref, n2_ref, w_in_ref, w_out_ref, np_ref, w_gate_ref, w_proj_ref, nf_ref,
                 o_ref, act_ref, sb_ref, acc_ref, decay_ref, need_ref, kbuf, vbuf, sems,
                 *, final, tiles_per_seq):
    j = pl.program_id(0)
    n_tiles = pl.num_programs(0) - 1
    cur = j % 2

    @pl.when(j == 0)
    def _():
        sb_ref[1] = jnp.zeros(sb_ref.shape[1:], sb_ref.dtype)

    tile = jnp.minimum(j, n_tiles - 1)
    b, qi = tile // tiles_per_seq, tile % tiles_per_seq
    attention = _attention_windows(q_ref, kp_ref, ko_ref, vp_ref, vo_ref, sb_ref.at[cur],
                                   acc_ref, decay_ref, need_ref, qi, ATTN_GROUPS)

    mixed = jnp.concatenate([gm_ref[...], sb_ref[1 - cur]], axis=1)
    h = h_ref[...] + _dot(mixed, w_mo_ref[...])
    y = _rms(h, n2_ref[...]).astype(BF16)
    h = h + 0.5 * _half_swiglu(y, w_in_ref, w_out_ref, act_ref, attention)
    gate = jax.nn.sigmoid(_dot(_rms(h, np_ref[...]).astype(BF16), w_gate_ref[...]))
    h = h + gate * _dot(p_ref[...].astype(BF16), w_proj_ref[...])
    o_ref[...] = _rms(h, nf_ref[...]) if final else h

    _attention_sweeps(q_ref, kp_ref, ko_ref, vp_ref, vo_ref, k_any, v_any, sb_ref.at[cur],
                      acc_ref, decay_ref, need_ref, kbuf, vbuf, sems, b, qi)


def _post_call(h1, gm, p2, q, k, v, w_mo, n2, w_in, w_out, np_, w_gate, w_proj, nf, final):
    tokens = h1.shape[0]
    batch, seq, _ = q.shape
    tm = TOKEN_TILE
    n_tiles = tokens // tm
    tiles_per_seq = seq // tm
    sub_per_tile = tm // ATTN_SUB
    dense = lambda width: pl.BlockSpec((tm, width), lambda j: (jnp.maximum(j - 1, 0), 0))

    def own(j):
        t = jnp.minimum(j, n_tiles - 1)
        return t // tiles_per_seq, t % tiles_per_seq, 0

    def prev(j):
        b, qi, _ = own(j)
        return b, jnp.maximum(qi * (sub_per_tile // 2) - 1, 0), 0

    own_spec = pl.BlockSpec((1, tm, SB_WIDTH), own)
    prev_spec = pl.BlockSpec((1, 2 * ATTN_SUB, SB_WIDTH), prev)
    hbm = pl.BlockSpec(memory_space=pl.ANY)
    return pl.pallas_call(
        functools.partial(_post_kernel, final=final, tiles_per_seq=tiles_per_seq),
        grid=(n_tiles + 1,),
        in_specs=[
            dense(D_MODEL), dense(GM_WIDTH), dense(PLE_DIM),
            own_spec, prev_spec, own_spec, prev_spec, own_spec, hbm, hbm,
            _resident(w_mo.shape),
            _resident((1, D_MODEL)),
            _resident(w_in.shape),
            _resident(w_out.shape),
            _resident((1, D_MODEL)),
            _resident(w_gate.shape),
            _resident(w_proj.shape),
            _resident((1, D_MODEL)),
        ],
        out_specs=dense(D_MODEL),
        out_shape=jax.ShapeDtypeStruct((tokens, D_MODEL), F32),
        scratch_shapes=[
            pltpu.VMEM((tm, D_FF), BF16),
            pltpu.VMEM((2, tm, SB_WIDTH), BF16),
            pltpu.VMEM((tm, SB_WIDTH), F32),
            pltpu.VMEM((sub_per_tile, 2 * ATTN_SUB, LANES), F32),
            pltpu.SMEM((SB_WIDTH // LANES * sub_per_tile,), jnp.int32),
            pltpu.VMEM((ATTN_SUB, LANES), BF16),
            pltpu.VMEM((ATTN_SUB, LANES), BF16),
            pltpu.SemaphoreType.DMA((2,)),
        ],
        compiler_params=pltpu.CompilerParams(
            dimension_semantics=("arbitrary",), vmem_limit_bytes=VMEM_LIMIT),
        name="post_attn_mix_out_ffn_ple",
    )(h1, gm, p2, q, k, k, v, v, k, v, w_mo, n2, w_in, w_out, np_, w_gate, w_proj, nf)


def kernel(x, p, ffn1_norm, ffn1_w_in, ffn1_w_out, mix_norm, w_mix_in, gmlp_v_norm, gmlp_w_s, gmlp_b, w_mix_out, ffn2_norm, ffn2_w_in, ffn2_w_out, ple_norm, ple_w_gate, ple_w_proj, final_norm):
    batch, seq, _ = x.shape
    depth = p.shape[0]
    tokens = batch * seq
    vec = lambda g: g.reshape(1, -1).astype(F32)

    h = x.reshape(tokens, D_MODEL)
    for i in range(depth):
        bs_full = jnp.repeat(gmlp_b[i].T, GM_HEAD_DIM, axis=1)
        h1, gm, q, k, v = _pre_call(
            h, vec(ffn1_norm[i]), ffn1_w_in[i].astype(BF16), ffn1_w_out[i].astype(BF16),
            vec(mix_norm[i]), w_mix_in[i].astype(BF16), vec(gmlp_v_norm[i]),
            gmlp_w_s[i], bs_full)
        shape3 = (batch, seq, SB_WIDTH)
        h = _post_call(
            h1, gm, p[i].reshape(tokens, PLE_DIM),
            q.reshape(shape3), k.reshape(shape3), v.reshape(shape3),
            w_mix_out[i].astype(BF16), vec(ffn2_norm[i]), ffn2_w_in[i].astype(BF16),
            ffn2_w_out[i].astype(BF16), vec(ple_norm[i]), ple_w_gate[i].astype(BF16),
            ple_w_proj[i].astype(BF16), vec(final_norm), final=(i == depth - 1))
    return h.reshape(batch, seq, D_MODEL)
```

```python
import functools

import jax
import jax.numpy as jnp
from jax import lax
from jax.experimental import pallas as pl
from jax.experimental.pallas import tpu as pltpu

D_MODEL = 1024
PLE_DIM = 256
D_FF = 2816
CHUNK = 128
GM_HEADS = 4
GM_HEAD_DIM = 128
GM_WIDTH = GM_HEADS * GM_HEAD_DIM
SB_HEADS = 8
SB_HEAD_DIM = 64
SB_WIDTH = SB_HEADS * SB_HEAD_DIM
EPS = 1e-6

LANES = 128
FF_CHUNK = 256
TOKEN_TILE = 512
CAST_ROWS = 16
ATTN_SUB = 128
ATTN_GROUPS = 8
LOG2E = 1.4426950408889634
DECAY_LIMIT = 127.0
VMEM_LIMIT = 56 * 1024 * 1024

BF16 = jnp.bfloat16
F32 = jnp.float32


def _dot(a, b):
    return jnp.dot(a, b, preferred_element_type=F32)


def _rms(x, g):
    return x * lax.rsqrt(jnp.mean(x * x, axis=-1, keepdims=True) + EPS) * g


def _half_swiglu(y, w_in_ref, w_out_ref, act_ref, side_work=(), first_side_chunk=0):
    for c in range(D_FF // FF_CHUNK):
        if 0 <= c - first_side_chunk < len(side_work):
            side_work[c - first_side_chunk]()
        lo = c * FF_CHUNK
        gate = _dot(y, w_in_ref[:, lo:lo + FF_CHUNK])
        up = _dot(y, w_in_ref[:, D_FF + lo:D_FF + lo + FF_CHUNK])
        act_ref[:, lo:lo + FF_CHUNK] = (gate * jax.nn.sigmoid(gate) * up).astype(BF16)
    return _dot(act_ref[...], w_out_ref[...])


def _gelu(x):
    return 0.5 * x * (1.0 + lax.erf(x * (2.0 ** -0.5)))


def _pre_kernel(x_ref, n1_ref, w_in_ref, w_out_ref, nm_ref, w_mix_ref, vg_ref, ws_ref, bs_ref,
                *refs):
    n_cast = (len(refs) - 6) // 2
    cast_in, refs = refs[:n_cast], refs[n_cast:]
    h_ref, gm_ref, q_ref, k_ref, v_ref = refs[:5]
    cast_out, act_ref = refs[5:5 + n_cast], refs[-1]
    for src, dst in zip(cast_in, cast_out):
        dst[...] = src[...].astype(BF16)

    x = x_ref[...]
    y = _rms(x, n1_ref[...]).astype(BF16)
    h = x + 0.5 * _half_swiglu(y, w_in_ref, w_out_ref, act_ref)
    h_ref[...] = h

    n = _rms(h, nm_ref[...]).astype(BF16)
    zu = _dot(n, w_mix_ref[:, 0:GM_WIDTH])
    zv = _dot(n, w_mix_ref[:, GM_WIDTH:2 * GM_WIDTH])

    base = 2 * GM_WIDTH
    q_ref[...] = (_dot(n, w_mix_ref[:, base:base + SB_WIDTH])
                  * (SB_HEAD_DIM ** -0.5 * LOG2E)).astype(BF16)
    k_ref[...] = _dot(n, w_mix_ref[:, base + SB_WIDTH:base + 2 * SB_WIDTH]).astype(BF16)
    v_ref[...] = _dot(n, w_mix_ref[:, base + 2 * SB_WIDTH:base + 3 * SB_WIDTH]).astype(BF16)

    u = _gelu(zu)
    v = _gelu(zv)
    vn = _rms(v, vg_ref[...]).astype(BF16)

    t_idx = lax.broadcasted_iota(jnp.int32, (CHUNK, CHUNK), 0)
    s_idx = lax.broadcasted_iota(jnp.int32, (CHUNK, CHUNK), 1)
    keep = s_idx <= t_idx
    for hd in range(GM_HEADS):
        w = jnp.where(keep, ws_ref[hd], 0.0).astype(BF16)
        cols = slice(hd * GM_HEAD_DIM, (hd + 1) * GM_HEAD_DIM)
        chunks = [slice(c * CHUNK, (c + 1) * CHUNK) for c in range(x.shape[0] // CHUNK)]
        sv = _dot(w, jnp.concatenate([vn[rows, cols] for rows in chunks], axis=1))
        for rows in chunks:
            gm_ref[rows, cols] = (u[rows, cols] * (sv[:, rows] + bs_ref[:, cols])).astype(BF16)


def _resident(shape):
    return pl.BlockSpec(shape, lambda *_: (0,) * len(shape), pipeline_mode=pl.Buffered(1))


def _pre_call(x2, n1, w_in, w_out, nm, w_mix, vg, ws, bs_full, next_weights):
    tokens = x2.shape[0]
    tm = TOKEN_TILE
    steps = tokens // tm
    row = lambda width: pl.BlockSpec((tm, width), lambda i: (i, 0))
    flat = [w.reshape(steps * CAST_ROWS, -1) for w in next_weights]
    cast_specs = [pl.BlockSpec((CAST_ROWS, w.shape[1]), lambda i: (i, 0)) for w in flat]
    outs = pl.pallas_call(
        _pre_kernel,
        grid=(steps,),
        in_specs=[
            row(D_MODEL),
            _resident((1, D_MODEL)),
            _resident(w_in.shape),
            _resident(w_out.shape),
            _resident((1, D_MODEL)),
            _resident(w_mix.shape),
            _resident((1, GM_WIDTH)),
            _resident(ws.shape),
            _resident(bs_full.shape),
        ] + cast_specs,
        out_specs=[row(D_MODEL), row(GM_WIDTH), row(SB_WIDTH), row(SB_WIDTH), row(SB_WIDTH)]
        + cast_specs,
        out_shape=[
            jax.ShapeDtypeStruct((tokens, D_MODEL), F32),
            jax.ShapeDtypeStruct((tokens, GM_WIDTH), BF16),
            jax.ShapeDtypeStruct((tokens, SB_WIDTH), BF16),
            jax.ShapeDtypeStruct((tokens, SB_WIDTH), BF16),
            jax.ShapeDtypeStruct((tokens, SB_WIDTH), BF16),
        ] + [jax.ShapeDtypeStruct(w.shape, BF16) for w in flat],
        scratch_shapes=[pltpu.VMEM((tm, D_FF), BF16)],
        compiler_params=pltpu.CompilerParams(
            dimension_semantics=("arbitrary",), vmem_limit_bytes=VMEM_LIMIT),
        name="pre_ffn_mix_in",
    )(x2, n1, w_in, w_out, nm, w_mix, vg, ws, bs_full, *flat)
    return list(outs[:5]) + [o.reshape(w.shape) for o, w in zip(outs[5:], next_weights)]


def _later_matrix(n):
    j = lax.broadcasted_iota(jnp.int32, (2 * n, n), 0) % n
    s = lax.broadcasted_iota(jnp.int32, (2 * n, n), 1)
    return (j > s).astype(BF16)


def _stack_heads(x):
    first = lax.broadcasted_iota(jnp.int32, x.shape, 1) < SB_HEAD_DIM
    zero = jnp.zeros_like(x)
    return jnp.concatenate([jnp.where(first, x, zero), jnp.where(first, zero, x)], axis=0)


def _scores(qs, kb):
    return lax.dot_general(qs, kb, (((1,), (1,)), ((), ())), preferred_element_type=F32)


def _softplus2(z):
    return jnp.maximum(z, 0.0) + jnp.log2(1.0 + jnp.exp2(-jnp.abs(z)))


def _mask_window(x, diff, left_valid):
    left, right = x[:, :ATTN_SUB], x[:, ATTN_SUB:]
    if left_valid is not True:
        left = jnp.where(left_valid, left, 0.0)
    right = jnp.where(diff < 0, right, 0.0)
    return jnp.concatenate([left, right], axis=1)


def _split(sp):
    hi = sp.astype(BF16)
    lo = (sp - hi.astype(F32)).astype(BF16)
    return jnp.concatenate([hi, lo], axis=1)


def _later_sum(sp, later):
    return _dot(_split(sp), later)


def _weights(z, sp, decay):
    return jnp.exp2((z - sp) - decay)


def _apply(a, vb):
    pv = _dot(a.astype(BF16), vb)
    first = lax.broadcasted_iota(jnp.int32, (ATTN_SUB, LANES), 1) < SB_HEAD_DIM
    return jnp.where(first, pv[:ATTN_SUB], pv[ATTN_SUB:])


def _window_consts():
    sub = ATTN_SUB
    diff = (lax.broadcasted_iota(jnp.int32, (2 * sub, sub), 1)
            - lax.broadcasted_iota(jnp.int32, (2 * sub, sub), 0) % sub)
    return _later_matrix(2 * sub), diff


def _attention_windows(q_ref, kp_ref, ko_ref, vp_ref, vo_ref, out_ref, acc_ref, decay_ref,
                       need_ref, qi, n_groups):
    sub = ATTN_SUB
    n_sub = q_ref.shape[1] // sub
    all_units = [(g, s) for g in range(SB_WIDTH // LANES) for s in range(n_sub)]
    per_group = len(all_units) // n_groups
    lanes = lambda g: slice(g * LANES, (g + 1) * LANES)

    def window(prev_ref, own_ref, g, s):
        if s == 0:
            return jnp.concatenate([prev_ref[0, sub:2 * sub, lanes(g)],
                                    own_ref[0, 0:sub, lanes(g)]], axis=0)
        return own_ref[0, (s - 1) * sub:(s + 1) * sub, lanes(g)]

    later_win, diff = _window_consts()
    groups = [all_units[k * per_group:(k + 1) * per_group] for k in range(n_groups)]
    left_valid = [[qi > 0 if s == 0 else True for _, s in units] for units in groups]
    state = [{} for _ in groups]

    def stage_scores(k):
        units, st = groups[k], state[k]
        st["z"] = [_scores(_stack_heads(q_ref[0, s * sub:(s + 1) * sub, lanes(g)]),
                           window(kp_ref, ko_ref, g, s)) for g, s in units]
        st["sp"] = [_softplus2(x) for x in st["z"]]
        spm = [_mask_window(x, diff, lv) for x, lv in zip(st["sp"], left_valid[k])]
        st["split"] = [_split(x) for x in spm]
        for u, (g, s) in enumerate(units):
            decay = jnp.sum(spm[u], axis=1, keepdims=True)
            decay_ref[s, :, g:g + 1] = decay
            unfinished = jnp.logical_and(jnp.min(decay) < DECAY_LIMIT, qi * n_sub + s >= 2)
            need_ref[k * per_group + u] = unfinished.astype(jnp.int32)

    def stage_later(k):
        st = state[k]
        stacked = _dot(jnp.concatenate(st["split"], axis=0), later_win)
        later = [stacked[u * 2 * sub:(u + 1) * 2 * sub] for u in range(per_group)]
        st["a"] = [_mask_window(_weights(*zsl), diff, lv).astype(BF16)
                   for zsl, lv in zip(zip(st["z"], st["sp"], later), left_valid[k])]

    def stage_values(k):
        for a, (g, s) in zip(state[k]["a"], groups[k]):
            pv = _apply(a, window(vp_ref, vo_ref, g, s))
            acc_ref[s * sub:(s + 1) * sub, lanes(g)] = pv
            out_ref[s * sub:(s + 1) * sub, lanes(g)] = pv.astype(out_ref.dtype)

    def thunk(t):
        for stage, k in ((stage_values, t - 2), (stage_later, t - 1), (stage_scores, t)):
            if 0 <= k < n_groups:
                stage(k)

    return [functools.partial(thunk, t) for t in range(n_groups + 2)]


def _attention_sweeps(q_ref, kp_ref, ko_ref, vp_ref, vo_ref, k_any, v_any, out_ref, acc_ref,
                      decay_ref, need_ref, kbuf, vbuf, sems, b, qi):
    sub = ATTN_SUB
    n_sub = q_ref.shape[1] // sub
    later_sub = _later_matrix(sub)

    for g in range(SB_WIDTH // LANES):
        lanes = slice(g * LANES, (g + 1) * LANES)

        def fetch(first_key):
            copies = [pltpu.make_async_copy(src.at[b, pl.ds(first_key, sub), lanes], dst, sems.at[i])
                      for i, (src, dst) in enumerate(((k_any, kbuf), (v_any, vbuf)))]
            for c in copies:
                c.start()
            for c in copies:
                c.wait()

        def per_sub(s, carry):
            @pl.when(need_ref[g * n_sub + s] != 0)
            def _():
                tile = qi * n_sub + s
                rows = pl.ds(pl.multiple_of(s * sub, sub), sub)
                qs = _stack_heads(q_ref[0, rows, lanes])

                def block(kb, vb, acc, decay):
                    z = _scores(qs, kb)
                    sp = _softplus2(z)
                    a = _weights(z, sp, _later_sum(sp, later_sub) + decay)
                    return acc + _apply(a, vb), decay + jnp.sum(sp, axis=1, keepdims=True)

                def resident(prev_ref, own_ref):
                    in_own = pl.ds(pl.multiple_of(jnp.maximum(s - 2, 0) * sub, sub), sub)
                    in_prev = pl.ds(pl.multiple_of(jnp.minimum(s, 1) * sub, sub), sub)
                    return lax.cond(s >= 2, lambda: own_ref[0, in_own, lanes],
                                    lambda: prev_ref[0, in_prev, lanes])

                acc, decay = block(resident(kp_ref, ko_ref), resident(vp_ref, vo_ref),
                                   acc_ref[rows, lanes], decay_ref[s, :, g:g + 1])

                def cond(st):
                    return jnp.logical_and(st[0] >= 0, st[1])

                def body(st):
                    j, _, acc, decay = st
                    fetch(pl.multiple_of(j * sub, sub))
                    acc, decay = block(kbuf[...], vbuf[...], acc, decay)
                    return j - 1, jnp.min(decay) < DECAY_LIMIT, acc, decay

                st = lax.while_loop(cond, body,
                                    (tile - 3, jnp.min(decay) < DECAY_LIMIT, acc, decay))
                out_ref[rows, lanes] = st[2].astype(out_ref.dtype)

            return carry

        lax.fori_loop(0, n_sub, per_sub, 0)


def _post_kernel(h_ref, gm_ref, p_ref, q_ref, kp_ref, ko_ref, vp_ref, vo_ref, k_any, v_any,
                 w_mo_ref, n2_ref, w_in_ref, w_out_ref, np_ref, w_gate_ref, w_proj_ref, nf_ref,
                 o_ref, act_ref, sb_ref, acc_ref, decay_ref, need_ref, kbuf, vbuf, sems,
                 *, final, tiles_per_seq):
    j = pl.program_id(0)
    n_tiles = pl.num_programs(0) - 1
    cur = j % 2

    @pl.when(j == 0)
    def _():
        sb_ref[1] = jnp.zeros(sb_ref.shape[1:], sb_ref.dtype)

    tile = jnp.minimum(j, n_tiles - 1)
    b, qi = tile // tiles_per_seq, tile % tiles_per_seq
    attention = _attention_windows(q_ref, kp_ref, ko_ref, vp_ref, vo_ref, sb_ref.at[cur],
                                   acc_ref, decay_ref, need_ref, qi, ATTN_GROUPS)

    mixed = jnp.concatenate([gm_ref[...], sb_ref[1 - cur]], axis=1)
    h = h_ref[...] + _dot(mixed, w_mo_ref[...])
    y = _rms(h, n2_ref[...]).astype(BF16)
    h = h + 0.5 * _half_swiglu(y, w_in_ref, w_out_ref, act_ref, attention[:-2])
    attention[-2]()
    gate = jax.nn.sigmoid(_dot(_rms(h, np_ref[...]).astype(BF16), w_gate_ref[...]))
    attention[-1]()
    h = h + gate * _dot(p_ref[...].astype(BF16), w_proj_ref[...])
    o_ref[...] = _rms(h, nf_ref[...]) if final else h

    _attention_sweeps(q_ref, kp_ref, ko_ref, vp_ref, vo_ref, k_any, v_any, sb_ref.at[cur],
                      acc_ref, decay_ref, need_ref, kbuf, vbuf, sems, b, qi)


def _post_call(h1, gm, p2, q, k, v, w_mo, n2, w_in, w_out, np_, w_gate, w_proj, nf, final):
    tokens = h1.shape[0]
    batch, seq, _ = q.shape
    tm = TOKEN_TILE
    n_tiles = tokens // tm
    tiles_per_seq = seq // tm
    sub_per_tile = tm // ATTN_SUB
    dense = lambda width: pl.BlockSpec((tm, width), lambda j: (jnp.maximum(j - 1, 0), 0))

    def own(j):
        t = jnp.minimum(j, n_tiles - 1)
        return t // tiles_per_seq, t % tiles_per_seq, 0

    def prev(j):
        b, qi, _ = own(j)
        return b, jnp.maximum(qi * (sub_per_tile // 2) - 1, 0), 0

    own_spec = pl.BlockSpec((1, tm, SB_WIDTH), own)
    prev_spec = pl.BlockSpec((1, 2 * ATTN_SUB, SB_WIDTH), prev)
    hbm = pl.BlockSpec(memory_space=pl.ANY)
    return pl.pallas_call(
        functools.partial(_post_kernel, final=final, tiles_per_seq=tiles_per_seq),
        grid=(n_tiles + 1,),
        in_specs=[
            dense(D_MODEL), dense(GM_WIDTH), dense(PLE_DIM),
            own_spec, prev_spec, own_spec, prev_spec, own_spec, hbm, hbm,
            _resident(w_mo.shape),
            _resident((1, D_MODEL)),
            _resident(w_in.shape),
            _resident(w_out.shape),
            _resident((1, D_MODEL)),
            _resident(w_gate.shape),
            _resident(w_proj.shape),
            _resident((1, D_MODEL)),
        ],
        out_specs=dense(D_MODEL),
        out_shape=jax.ShapeDtypeStruct((tokens, D_MODEL), F32),
        scratch_shapes=[
            pltpu.VMEM((tm, D_FF), BF16),
            pltpu.VMEM((2, tm, SB_WIDTH), BF16),
            pltpu.VMEM((tm, SB_WIDTH), F32),
            pltpu.VMEM((sub_per_tile, 2 * ATTN_SUB, LANES), F32),
            pltpu.SMEM((SB_WIDTH // LANES * sub_per_tile,), jnp.int32),
            pltpu.VMEM((ATTN_SUB, LANES), BF16),
            pltpu.VMEM((ATTN_SUB, LANES), BF16),
            pltpu.SemaphoreType.DMA((2,)),
        ],
        compiler_params=pltpu.CompilerParams(
            dimension_semantics=("arbitrary",), vmem_limit_bytes=VMEM_LIMIT),
        name="post_attn_mix_out_ffn_ple",
    )(h1, gm, p2, q, k, k, v, v, k, v, w_mo, n2, w_in, w_out, np_, w_gate, w_proj, nf)


def kernel(x, p, ffn1_norm, ffn1_w_in, ffn1_w_out, mix_norm, w_mix_in, gmlp_v_norm, gmlp_w_s, gmlp_b, w_mix_out, ffn2_norm, ffn2_w_in, ffn2_w_out, ple_norm, ple_w_gate, ple_w_proj, final_norm):
    batch, seq, _ = x.shape
    depth = p.shape[0]
    tokens = batch * seq
    vec = lambda g: g.reshape(1, -1).astype(F32)

    h = x.reshape(tokens, D_MODEL)
    for i in range(depth):
        bs_full = jnp.repeat(gmlp_b[i].T, GM_HEAD_DIM, axis=1)
        h1, gm, q, k, v, w_mo, w_in2, w_out2, w_gate, w_proj = _pre_call(
            h, vec(ffn1_norm[i]), ffn1_w_in[i].astype(BF16), ffn1_w_out[i].astype(BF16),
            vec(mix_norm[i]), w_mix_in[i].astype(BF16), vec(gmlp_v_norm[i]),
            gmlp_w_s[i], bs_full,
            [w_mix_out[i], ffn2_w_in[i], ffn2_w_out[i], ple_w_gate[i], ple_w_proj[i]])
        shape3 = (batch, seq, SB_WIDTH)
        h = _post_call(
            h1, gm, p[i].reshape(tokens, PLE_DIM),
            q.reshape(shape3), k.reshape(shape3), v.reshape(shape3),
            w_mo, vec(ffn2_norm[i]), w_in2, w_out2, vec(ple_norm[i]), w_gate, w_proj,
            vec(final_norm), final=(i == depth - 1))
    return h.reshape(batch, seq, D_MODEL)
```

```python
import functools

import jax
import jax.numpy as jnp
from jax import lax
from jax.experimental import pallas as pl
from jax.experimental.pallas import tpu as pltpu

D_MODEL = 1024
PLE_DIM = 256
D_FF = 2816
CHUNK = 128
GM_HEADS = 4
GM_HEAD_DIM = 128
GM_WIDTH = GM_HEADS * GM_HEAD_DIM
SB_HEADS = 8
SB_HEAD_DIM = 64
SB_WIDTH = SB_HEADS * SB_HEAD_DIM
EPS = 1e-6

LANES = 128
FF_CHUNK = 256
TOKEN_TILE = 512
CAST_ROWS = 16
ATTN_SUB = 128
ATTN_GROUPS = 8
LOG2E = 1.4426950408889634
DECAY_LIMIT = 127.0
VMEM_LIMIT = 56 * 1024 * 1024

BF16 = jnp.bfloat16
F32 = jnp.float32


def _dot(a, b):
    return jnp.dot(a, b, preferred_element_type=F32)


def _rms(x, g):
    return x * lax.rsqrt(jnp.mean(x * x, axis=-1, keepdims=True) + EPS) * g


def _half_swiglu(y, w_in_ref, w_out_ref, act_ref, side_work=(), first_side_chunk=0):
    for c in range(D_FF // FF_CHUNK):
        if 0 <= c - first_side_chunk < len(side_work):
            side_work[c - first_side_chunk]()
        lo = c * FF_CHUNK
        gate = _dot(y, w_in_ref[:, lo:lo + FF_CHUNK])
        up = _dot(y, w_in_ref[:, D_FF + lo:D_FF + lo + FF_CHUNK])
        act_ref[:, lo:lo + FF_CHUNK] = (gate * jax.nn.sigmoid(gate) * up).astype(BF16)
    return _dot(act_ref[...], w_out_ref[...])


def _gelu(x):
    return 0.5 * x * (1.0 + lax.erf(x * (2.0 ** -0.5)))


def _pre_kernel(x_ref, n1_ref, w_in_ref, w_out_ref, nm_ref, w_mix_ref, vg_ref, ws_ref, bs_ref,
                *refs):
    n_cast = (len(refs) - 6) // 2
    cast_in, refs = refs[:n_cast], refs[n_cast:]
    h_ref, gm_ref, q_ref, k_ref, v_ref = refs[:5]
    cast_out, act_ref = refs[5:5 + n_cast], refs[-1]
    for src, dst in zip(cast_in, cast_out):
        dst[...] = src[...].astype(BF16)

    x = x_ref[...]
    y = _rms(x, n1_ref[...]).astype(BF16)
    h = x + 0.5 * _half_swiglu(y, w_in_ref, w_out_ref, act_ref)
    h_ref[...] = h

    n = _rms(h, nm_ref[...]).astype(BF16)
    zu = _dot(n, w_mix_ref[:, 0:GM_WIDTH])
    zv = _dot(n, w_mix_ref[:, GM_WIDTH:2 * GM_WIDTH])

    base = 2 * GM_WIDTH
    q_ref[...] = (_dot(n, w_mix_ref[:, base:base + SB_WIDTH])
                  * (SB_HEAD_DIM ** -0.5 * LOG2E)).astype(BF16)
    k_ref[...] = _dot(n, w_mix_ref[:, base + SB_WIDTH:base + 2 * SB_WIDTH]).astype(BF16)
    v_ref[...] = _dot(n, w_mix_ref[:, base + 2 * SB_WIDTH:base + 3 * SB_WIDTH]).astype(BF16)

    u = _gelu(zu)
    v = _gelu(zv)
    vn = _rms(v, vg_ref[...]).astype(BF16)

    t_idx = lax.broadcasted_iota(jnp.int32, (CHUNK, CHUNK), 0)
    s_idx = lax.broadcasted_iota(jnp.int32, (CHUNK, CHUNK), 1)
    keep = s_idx <= t_idx
    for hd in range(GM_HEADS):
        w = jnp.where(keep, ws_ref[hd], 0.0).astype(BF16)
        cols = slice(hd * GM_HEAD_DIM, (hd + 1) * GM_HEAD_DIM)
        chunks = [slice(c * CHUNK, (c + 1) * CHUNK) for c in range(x.shape[0] // CHUNK)]
        sv = _dot(w, jnp.concatenate([vn[rows, cols] for rows in chunks], axis=1))
        for rows in chunks:
            gm_ref[rows, cols] = (u[rows, cols] * (sv[:, rows] + bs_ref[:, cols])).astype(BF16)


def _resident(shape):
    return pl.BlockSpec(shape, lambda *_: (0,) * len(shape), pipeline_mode=pl.Buffered(1))


def _cast_blocks(rows, steps):
    tiles = rows // CAST_ROWS
    return max(d for d in range(1, min(steps, tiles) + 1) if tiles % d == 0)


def _pre_call(x2, n1, w_in, w_out, nm, w_mix, vg, ws, bs_full, next_weights, layer):
    tokens = x2.shape[0]
    tm = TOKEN_TILE
    steps = tokens // tm
    row = lambda width: pl.BlockSpec((tm, width), lambda i: (i, 0))
    cast_in, cast_out = [], []
    for w in next_weights:
        _, rows, cols = w.shape
        blocks = _cast_blocks(rows, steps)
        cast_in.append(pl.BlockSpec(
            (None, rows // blocks, cols),
            lambda i, last=blocks - 1: (layer, jnp.minimum(i, last), 0)))
        cast_out.append(pl.BlockSpec(
            (rows // blocks, cols), lambda i, last=blocks - 1: (jnp.minimum(i, last), 0)))
    return pl.pallas_call(
        _pre_kernel,
        grid=(steps,),
        in_specs=[
            row(D_MODEL),
            _resident((1, D_MODEL)),
            _resident(w_in.shape),
            _resident(w_out.shape),
            _resident((1, D_MODEL)),
            _resident(w_mix.shape),
            _resident((1, GM_WIDTH)),
            _resident(ws.shape),
            _resident(bs_full.shape),
        ] + cast_in,
        out_specs=[row(D_MODEL), row(GM_WIDTH), row(SB_WIDTH), row(SB_WIDTH), row(SB_WIDTH)]
        + cast_out,
        out_shape=[
            jax.ShapeDtypeStruct((tokens, D_MODEL), F32),
            jax.ShapeDtypeStruct((tokens, GM_WIDTH), BF16),
            jax.ShapeDtypeStruct((tokens, SB_WIDTH), BF16),
            jax.ShapeDtypeStruct((tokens, SB_WIDTH), BF16),
            jax.ShapeDtypeStruct((tokens, SB_WIDTH), BF16),
        ] + [jax.ShapeDtypeStruct(w.shape[1:], BF16) for w in next_weights],
        scratch_shapes=[pltpu.VMEM((tm, D_FF), BF16)],
        compiler_params=pltpu.CompilerParams(
            dimension_semantics=("arbitrary",), vmem_limit_bytes=VMEM_LIMIT),
        name="pre_ffn_mix_in",
    )(x2, n1, w_in, w_out, nm, w_mix, vg, ws, bs_full, *next_weights)


def _later_matrix(n):
    j = lax.broadcasted_iota(jnp.int32, (2 * n, n), 0) % n
    s = lax.broadcasted_iota(jnp.int32, (2 * n, n), 1)
    return (j > s).astype(BF16)


def _stack_heads(x):
    first = lax.broadcasted_iota(jnp.int32, x.shape, 1) < SB_HEAD_DIM
    zero = jnp.zeros_like(x)
    return jnp.concatenate([jnp.where(first, x, zero), jnp.where(first, zero, x)], axis=0)


def _scores(qs, kb):
    return lax.dot_general(qs, kb, (((1,), (1,)), ((), ())), preferred_element_type=F32)


def _softplus2(z):
    return jnp.maximum(z, 0.0) + jnp.log2(1.0 + jnp.exp2(-jnp.abs(z)))


def _mask_window(x, diff, left_valid):
    left, right = x[:, :ATTN_SUB], x[:, ATTN_SUB:]
    if left_valid is not True:
        left = jnp.where(left_valid, left, 0.0)
    right = jnp.where(diff < 0, right, 0.0)
    return jnp.concatenate([left, right], axis=1)


def _split(sp):
    hi = sp.astype(BF16)
    lo = (sp - hi.astype(F32)).astype(BF16)
    return jnp.concatenate([hi, lo], axis=1)


def _later_sum(sp, later):
    return _dot(_split(sp), later)


def _weights(z, sp, decay):
    return jnp.exp2((z - sp) - decay)


def _apply(a, vb):
    pv = _dot(a.astype(BF16), vb)
    first = lax.broadcasted_iota(jnp.int32, (ATTN_SUB, LANES), 1) < SB_HEAD_DIM
    return jnp.where(first, pv[:ATTN_SUB], pv[ATTN_SUB:])


def _window_consts():
    sub = ATTN_SUB
    diff = (lax.broadcasted_iota(jnp.int32, (2 * sub, sub), 1)
            - lax.broadcasted_iota(jnp.int32, (2 * sub, sub), 0) % sub)
    return _later_matrix(2 * sub), diff


def _attention_windows(q_ref, kp_ref, ko_ref, vp_ref, vo_ref, out_ref, acc_ref, decay_ref,
                       need_ref, qi, n_groups):
    sub = ATTN_SUB
    n_sub = q_ref.shape[1] // sub
    all_units = [(g, s) for g in range(SB_WIDTH // LANES) for s in range(n_sub)]
    per_group = len(all_units) // n_groups
    lanes = lambda g: slice(g * LANES, (g + 1) * LANES)

    def window(prev_ref, own_ref, g, s):
        if s == 0:
            return jnp.concatenate([prev_ref[0, sub:2 * sub, lanes(g)],
                                    own_ref[0, 0:sub, lanes(g)]], axis=0)
        return own_ref[0, (s - 1) * sub:(s + 1) * sub, lanes(g)]

    later_win, diff = _window_consts()
    groups = [all_units[k * per_group:(k + 1) * per_group] for k in range(n_groups)]
    left_valid = [[qi > 0 if s == 0 else True for _, s in units] for units in groups]
    state = [{} for _ in groups]

    def stage_scores(k):
        units, st = groups[k], state[k]
        st["z"] = [_scores(_stack_heads(q_ref[0, s * sub:(s + 1) * sub, lanes(g)]),
                           window(kp_ref, ko_ref, g, s)) for g, s in units]
        st["sp"] = [_softplus2(x) for x in st["z"]]
        spm = [_mask_window(x, diff, lv) for x, lv in zip(st["sp"], left_valid[k])]
        st["split"] = [_split(x) for x in spm]
        for u, (g, s) in enumerate(units):
            decay = jnp.sum(spm[u], axis=1, keepdims=True)
            decay_ref[s, :, g:g + 1] = decay
            unfinished = jnp.logical_and(jnp.min(decay) < DECAY_LIMIT, qi * n_sub + s >= 2)
            need_ref[k * per_group + u] = unfinished.astype(jnp.int32)

    def stage_later(k):
        st = state[k]
        stacked = _dot(jnp.concatenate(st["split"], axis=0), later_win)
        later = [stacked[u * 2 * sub:(u + 1) * 2 * sub] for u in range(per_group)]
        st["a"] = [_mask_window(_weights(*zsl), diff, lv).astype(BF16)
                   for zsl, lv in zip(zip(st["z"], st["sp"], later), left_valid[k])]

    def stage_values(k):
        for a, (g, s) in zip(state[k]["a"], groups[k]):
            pv = _apply(a, window(vp_ref, vo_ref, g, s))
            acc_ref[s * sub:(s + 1) * sub, lanes(g)] = pv
            out_ref[s * sub:(s + 1) * sub, lanes(g)] = pv.astype(out_ref.dtype)

    def thunk(t):
        for stage, k in ((stage_values, t - 2), (stage_later, t - 1), (stage_scores, t)):
            if 0 <= k < n_groups:
                stage(k)

    return [functools.partial(thunk, t) for t in range(n_groups + 2)]


def _attention_sweeps(q_ref, kp_ref, ko_ref, vp_ref, vo_ref, k_any, v_any, out_ref, acc_ref,
                      decay_ref, need_ref, kbuf, vbuf, sems, b, qi):
    sub = ATTN_SUB
    n_sub = q_ref.shape[1] // sub
    later_sub = _later_matrix(sub)

    for g in range(SB_WIDTH // LANES):
        lanes = slice(g * LANES, (g + 1) * LANES)

        def fetch(first_key):
            copies = [pltpu.make_async_copy(src.at[b, pl.ds(first_key, sub), lanes], dst, sems.at[i])
                      for i, (src, dst) in enumerate(((k_any, kbuf), (v_any, vbuf)))]
            for c in copies:
                c.start()
            for c in copies:
                c.wait()

        def per_sub(s, carry):
            @pl.when(need_ref[g * n_sub + s] != 0)
            def _():
                tile = qi * n_sub + s
                rows = pl.ds(pl.multiple_of(s * sub, sub), sub)
                qs = _stack_heads(q_ref[0, rows, lanes])

                def block(kb, vb, acc, decay):
                    z = _scores(qs, kb)
                    sp = _softplus2(z)
                    a = _weights(z, sp, _later_sum(sp, later_sub) + decay)
                    return acc + _apply(a, vb), decay + jnp.sum(sp, axis=1, keepdims=True)

                def resident(prev_ref, own_ref):
                    in_own = pl.ds(pl.multiple_of(jnp.maximum(s - 2, 0) * sub, sub), sub)
                    in_prev = pl.ds(pl.multiple_of(jnp.minimum(s, 1) * sub, sub), sub)
                    return lax.cond(s >= 2, lambda: own_ref[0, in_own, lanes],
                                    lambda: prev_ref[0, in_prev, lanes])

                acc, decay = block(resident(kp_ref, ko_ref), resident(vp_ref, vo_ref),
                                   acc_ref[rows, lanes], decay_ref[s, :, g:g + 1])

                def cond(st):
                    return jnp.logical_and(st[0] >= 0, st[1])

                def body(st):
                    j, _, acc, decay = st
                    fetch(pl.multiple_of(j * sub, sub))
                    acc, decay = block(kbuf[...], vbuf[...], acc, decay)
                    return j - 1, jnp.min(decay) < DECAY_LIMIT, acc, decay

                st = lax.while_loop(cond, body,
                                    (tile - 3, jnp.min(decay) < DECAY_LIMIT, acc, decay))
                out_ref[rows, lanes] = st[2].astype(out_ref.dtype)

            return carry

        lax.fori_loop(0, n_sub, per_sub, 0)


def _post_kernel(h_ref, gm_ref, p_ref, q_ref, kp_ref, ko_ref, vp_ref, vo_ref, k_any, v_any,
                 w_mo_ref, n2_ref, w_in_ref, w_out_ref, np_ref, w_gate_ref, w_proj_ref, nf_ref,
                 o_ref, act_ref, sb_ref, acc_ref, decay_ref, need_ref, kbuf, vbuf, sems,
                 *, final, tiles_per_seq):
    j = pl.program_id(0)
    n_tiles = pl.num_programs(0) - 1
    cur = j % 2

    @pl.when(j == 0)
    def _():
        sb_ref[1] = jnp.zeros(sb_ref.shape[1:], sb_ref.dtype)

    tile = jnp.minimum(j, n_tiles - 1)
    b, qi = tile // tiles_per_seq, tile % tiles_per_seq
    attention = _attention_windows(q_ref, kp_ref, ko_ref, vp_ref, vo_ref, sb_ref.at[cur],
                                   acc_ref, decay_ref, need_ref, qi, ATTN_GROUPS)

    mixed = jnp.concatenate([gm_ref[...], sb_ref[1 - cur]], axis=1)
    h = h_ref[...] + _dot(mixed, w_mo_ref[...])
    y = _rms(h, n2_ref[...]).astype(BF16)
    h = h + 0.5 * _half_swiglu(y, w_in_ref, w_out_ref, act_ref, attention[:-2])
    attention[-2]()
    gate = jax.nn.sigmoid(_dot(_rms(h, np_ref[...]).astype(BF16), w_gate_ref[...]))
    attention[-1]()
    h = h + gate * _dot(p_ref[...].astype(BF16), w_proj_ref[...])
    o_ref[...] = _rms(h, nf_ref[...]) if final else h

    _attention_sweeps(q_ref, kp_ref, ko_ref, vp_ref, vo_ref, k_any, v_any, sb_ref.at[cur],
                      acc_ref, decay_ref, need_ref, kbuf, vbuf, sems, b, qi)


def _post_call(h1, gm, p2, q, k, v, w_mo, n2, w_in, w_out, np_, w_gate, w_proj, nf, final):
    tokens = h1.shape[0]
    batch, seq, _ = q.shape
    tm = TOKEN_TILE
    n_tiles = tokens // tm
    tiles_per_seq = seq // tm
    sub_per_tile = tm // ATTN_SUB
    dense = lambda width: pl.BlockSpec((tm, width), lambda j: (jnp.maximum(j - 1, 0), 0))

    def own(j):
        t = jnp.minimum(j, n_tiles - 1)
        return t // tiles_per_seq, t % tiles_per_seq, 0

    def prev(j):
        b, qi, _ = own(j)
        return b, jnp.maximum(qi * (sub_per_tile // 2) - 1, 0), 0

    own_spec = pl.BlockSpec((1, tm, SB_WIDTH), own)
    prev_spec = pl.BlockSpec((1, 2 * ATTN_SUB, SB_WIDTH), prev)
    hbm = pl.BlockSpec(memory_space=pl.ANY)
    return pl.pallas_call(
        functools.partial(_post_kernel, final=final, tiles_per_seq=tiles_per_seq),
        grid=(n_tiles + 1,),
        in_specs=[
            dense(D_MODEL), dense(GM_WIDTH), dense(PLE_DIM),
            own_spec, prev_spec, own_spec, prev_spec, own_spec, hbm, hbm,
            _resident(w_mo.shape),
            _resident((1, D_MODEL)),
            _resident(w_in.shape),
            _resident(w_out.shape),
            _resident((1, D_MODEL)),
            _resident(w_gate.shape),
            _resident(w_proj.shape),
            _resident((1, D_MODEL)),
        ],
        out_specs=dense(D_MODEL),
        out_shape=jax.ShapeDtypeStruct((tokens, D_MODEL), F32),
        scratch_shapes=[
            pltpu.VMEM((tm, D_FF), BF16),
            pltpu.VMEM((2, tm, SB_WIDTH), BF16),
            pltpu.VMEM((tm, SB_WIDTH), F32),
            pltpu.VMEM((sub_per_tile, 2 * ATTN_SUB, LANES), F32),
            pltpu.SMEM((SB_WIDTH // LANES * sub_per_tile,), jnp.int32),
            pltpu.VMEM((ATTN_SUB, LANES), BF16),
            pltpu.VMEM((ATTN_SUB, LANES), BF16),
            pltpu.SemaphoreType.DMA((2,)),
        ],
        compiler_params=pltpu.CompilerParams(
            dimension_semantics=("arbitrary",), vmem_limit_bytes=VMEM_LIMIT),
        name="post_attn_mix_out_ffn_ple",
    )(h1, gm, p2, q, k, k, v, v, k, v, w_mo, n2, w_in, w_out, np_, w_gate, w_proj, nf)


def kernel(x, p, ffn1_norm, ffn1_w_in, ffn1_w_out, mix_norm, w_mix_in, gmlp_v_norm, gmlp_w_s, gmlp_b, w_mix_out, ffn2_norm, ffn2_w_in, ffn2_w_out, ple_norm, ple_w_gate, ple_w_proj, final_norm):
    batch, seq, _ = x.shape
    depth = p.shape[0]
    tokens = batch * seq
    vec = lambda g: g.reshape(1, -1).astype(F32)

    h = x.reshape(tokens, D_MODEL)
    for i in range(depth):
        bs_full = jnp.repeat(gmlp_b[i].T, GM_HEAD_DIM, axis=1)
        h1, gm, q, k, v, w_mo, w_in2, w_out2, w_gate, w_proj = _pre_call(
            h, vec(ffn1_norm[i]), ffn1_w_in[i].astype(BF16), ffn1_w_out[i].astype(BF16),
            vec(mix_norm[i]), w_mix_in[i].astype(BF16), vec(gmlp_v_norm[i]),
            gmlp_w_s[i], bs_full,
            [w_mix_out, ffn2_w_in, ffn2_w_out, ple_w_gate, ple_w_proj], layer=i)
        shape3 = (batch, seq, SB_WIDTH)
        h = _post_call(
            h1, gm, p[i].reshape(tokens, PLE_DIM),
            q.reshape(shape3), k.reshape(shape3), v.reshape(shape3),
            w_mo, vec(ffn2_norm[i]), w_in2, w_out2, vec(ple_norm[i]), w_gate, w_proj,
            vec(final_norm), final=(i == depth - 1))
    return h.reshape(batch, seq, D_MODEL)
```

```python
import functools

import jax
import jax.numpy as jnp
from jax import lax
from jax.experimental import pallas as pl
from jax.experimental.pallas import tpu as pltpu

D_MODEL = 1024
PLE_DIM = 256
D_FF = 2816
CHUNK = 128
GM_HEADS = 4
GM_HEAD_DIM = 128
GM_WIDTH = GM_HEADS * GM_HEAD_DIM
SB_HEADS = 8
SB_HEAD_DIM = 64
SB_WIDTH = SB_HEADS * SB_HEAD_DIM
EPS = 1e-6

LANES = 128
FF_CHUNK = 256
TOKEN_TILE = 512
CAST_ROWS = 16
ATTN_SUB = 128
ATTN_GROUPS = 8
LOG2E = 1.4426950408889634
DECAY_LIMIT = 127.0
VMEM_LIMIT = 56 * 1024 * 1024

BF16 = jnp.bfloat16
F32 = jnp.float32


def _dot(a, b):
    return jnp.dot(a, b, preferred_element_type=F32)


def _rms(x, g):
    return x * lax.rsqrt(jnp.mean(x * x, axis=-1, keepdims=True) + EPS) * g


def _half_swiglu(y, w_in_ref, w_out_ref, act_ref, side_work=(), first_side_chunk=0):
    for c in range(D_FF // FF_CHUNK):
        if 0 <= c - first_side_chunk < len(side_work):
            side_work[c - first_side_chunk]()
        lo = c * FF_CHUNK
        gate = _dot(y, w_in_ref[:, lo:lo + FF_CHUNK])
        up = _dot(y, w_in_ref[:, D_FF + lo:D_FF + lo + FF_CHUNK])
        act_ref[:, lo:lo + FF_CHUNK] = (gate * jax.nn.sigmoid(gate) * up).astype(BF16)
    return _dot(act_ref[...], w_out_ref[...])


def _gelu(x):
    return 0.5 * x * (1.0 + lax.erf(x * (2.0 ** -0.5)))


def _pre_kernel(x_ref, n1_ref, w_in_ref, w_out_ref, nm_ref, w_mix_ref, vg_ref, ws_ref, bs_ref,
                *refs):
    n_cast = (len(refs) - 6) // 2
    cast_in, refs = refs[:n_cast], refs[n_cast:]
    h_ref, gm_ref, q_ref, k_ref, v_ref = refs[:5]
    cast_out, act_ref = refs[5:5 + n_cast], refs[-1]
    for src, dst in zip(cast_in, cast_out):
        dst[...] = src[...].astype(BF16)

    x = x_ref[...]
    y = _rms(x, n1_ref[...]).astype(BF16)
    h = x + 0.5 * _half_swiglu(y, w_in_ref, w_out_ref, act_ref)
    h_ref[...] = h

    n = _rms(h, nm_ref[...]).astype(BF16)
    zu = _dot(n, w_mix_ref[:, 0:GM_WIDTH])
    zv = _dot(n, w_mix_ref[:, GM_WIDTH:2 * GM_WIDTH])

    base = 2 * GM_WIDTH
    q_ref[...] = (_dot(n, w_mix_ref[:, base:base + SB_WIDTH])
                  * (SB_HEAD_DIM ** -0.5 * LOG2E)).astype(BF16)
    k_ref[...] = _dot(n, w_mix_ref[:, base + SB_WIDTH:base + 2 * SB_WIDTH]).astype(BF16)
    v_ref[...] = _dot(n, w_mix_ref[:, base + 2 * SB_WIDTH:base + 3 * SB_WIDTH]).astype(BF16)

    u = _gelu(zu)
    v = _gelu(zv)
    vn = _rms(v, vg_ref[...]).astype(BF16)

    t_idx = lax.broadcasted_iota(jnp.int32, (CHUNK, CHUNK), 0)
    s_idx = lax.broadcasted_iota(jnp.int32, (CHUNK, CHUNK), 1)
    keep = s_idx <= t_idx
    for hd in range(GM_HEADS):
        w = jnp.where(keep, ws_ref[hd], 0.0).astype(BF16)
        cols = slice(hd * GM_HEAD_DIM, (hd + 1) * GM_HEAD_DIM)
        chunks = [slice(c * CHUNK, (c + 1) * CHUNK) for c in range(x.shape[0] // CHUNK)]
        sv = _dot(w, jnp.concatenate([vn[rows, cols] for rows in chunks], axis=1))
        for rows in chunks:
            gm_ref[rows, cols] = (u[rows, cols] * (sv[:, rows] + bs_ref[:, cols])).astype(BF16)


def _resident(shape):
    return pl.BlockSpec(shape, lambda *_: (0,) * len(shape), pipeline_mode=pl.Buffered(1))


def _cast_blocks(rows, steps):
    tiles = rows // CAST_ROWS
    return max(d for d in range(1, min(steps, tiles) + 1) if tiles % d == 0)


def _pre_call(x2, n1, w_in, w_out, nm, w_mix, vg, ws, bs_full, next_weights, layer):
    tokens = x2.shape[0]
    tm = TOKEN_TILE
    steps = tokens // tm
    row = lambda width: pl.BlockSpec((tm, width), lambda i: (i, 0))
    cast_in, cast_out = [], []
    for w in next_weights:
        _, rows, cols = w.shape
        blocks = _cast_blocks(rows, steps)
        cast_in.append(pl.BlockSpec(
            (None, rows // blocks, cols),
            lambda i, last=blocks - 1: (layer, jnp.minimum(i, last), 0)))
        cast_out.append(pl.BlockSpec(
            (rows // blocks, cols), lambda i, last=blocks - 1: (jnp.minimum(i, last), 0)))
    return pl.pallas_call(
        _pre_kernel,
        grid=(steps,),
        in_specs=[
            row(D_MODEL),
            _resident((1, D_MODEL)),
            _resident(w_in.shape),
            _resident(w_out.shape),
            _resident((1, D_MODEL)),
            _resident(w_mix.shape),
            _resident((1, GM_WIDTH)),
            _resident(ws.shape),
            _resident(bs_full.shape),
        ] + cast_in,
        out_specs=[row(D_MODEL), row(GM_WIDTH), row(SB_WIDTH), row(SB_WIDTH), row(SB_WIDTH)]
        + cast_out,
        out_shape=[
            jax.ShapeDtypeStruct((tokens, D_MODEL), F32),
            jax.ShapeDtypeStruct((tokens, GM_WIDTH), BF16),
            jax.ShapeDtypeStruct((tokens, SB_WIDTH), BF16),
            jax.ShapeDtypeStruct((tokens, SB_WIDTH), BF16),
            jax.ShapeDtypeStruct((tokens, SB_WIDTH), BF16),
        ] + [jax.ShapeDtypeStruct(w.shape[1:], BF16) for w in next_weights],
        scratch_shapes=[pltpu.VMEM((tm, D_FF), BF16)],
        compiler_params=pltpu.CompilerParams(
            dimension_semantics=("arbitrary",), vmem_limit_bytes=VMEM_LIMIT),
        name="pre_ffn_mix_in",
    )(x2, n1, w_in, w_out, nm, w_mix, vg, ws, bs_full, *next_weights)


def _later_matrix(n):
    j = lax.broadcasted_iota(jnp.int32, (n, n), 0)
    s = lax.broadcasted_iota(jnp.int32, (n, n), 1)
    return (j > s).astype(BF16)


def _stack_heads(x):
    first = lax.broadcasted_iota(jnp.int32, x.shape, 1) < SB_HEAD_DIM
    zero = jnp.zeros_like(x)
    return jnp.concatenate([jnp.where(first, x, zero), jnp.where(first, zero, x)], axis=0)


def _scores(qs, kb):
    return lax.dot_general(qs, kb, (((1,), (1,)), ((), ())), preferred_element_type=F32)


def _softplus2(z):
    return jnp.maximum(z, 0.0) + jnp.log2(1.0 + jnp.exp2(-jnp.abs(z)))


def _mask_window(x, diff, left_valid):
    left, right = x[:, :ATTN_SUB], x[:, ATTN_SUB:]
    if left_valid is not True:
        left = jnp.where(left_valid, left, 0.0)
    right = jnp.where(diff < 0, right, 0.0)
    return jnp.concatenate([left, right], axis=1)


def _later_sum(sp, later):
    return _dot(sp.astype(BF16), later)


def _weights(z, sp, decay):
    return jnp.exp2((z - sp) - decay)


def _apply(a, vb):
    pv = _dot(a.astype(BF16), vb)
    first = lax.broadcasted_iota(jnp.int32, (ATTN_SUB, LANES), 1) < SB_HEAD_DIM
    return jnp.where(first, pv[:ATTN_SUB], pv[ATTN_SUB:])


def _window_consts():
    sub = ATTN_SUB
    diff = (lax.broadcasted_iota(jnp.int32, (2 * sub, sub), 1)
            - lax.broadcasted_iota(jnp.int32, (2 * sub, sub), 0) % sub)
    return _later_matrix(2 * sub), diff


def _attention_windows(q_ref, kp_ref, ko_ref, vp_ref, vo_ref, out_ref, acc_ref, decay_ref,
                       need_ref, qi, n_groups):
    sub = ATTN_SUB
    n_sub = q_ref.shape[1] // sub
    all_units = [(g, s) for g in range(SB_WIDTH // LANES) for s in range(n_sub)]
    per_group = len(all_units) // n_groups
    lanes = lambda g: slice(g * LANES, (g + 1) * LANES)

    def window(prev_ref, own_ref, g, s):
        if s == 0:
            return jnp.concatenate([prev_ref[0, sub:2 * sub, lanes(g)],
                                    own_ref[0, 0:sub, lanes(g)]], axis=0)
        return own_ref[0, (s - 1) * sub:(s + 1) * sub, lanes(g)]

    later_win, diff = _window_consts()
    any_need = len(all_units)
    need_ref[any_need] = 0
    groups = [all_units[k * per_group:(k + 1) * per_group] for k in range(n_groups)]
    left_valid = [[qi > 0 if s == 0 else True for _, s in units] for units in groups]
    state = [{} for _ in groups]

    def stage_scores(k):
        units, st = groups[k], state[k]
        st["z"] = [_scores(_stack_heads(q_ref[0, s * sub:(s + 1) * sub, lanes(g)]),
                           window(kp_ref, ko_ref, g, s)) for g, s in units]
        st["sp"] = [_softplus2(x) for x in st["z"]]
        spm = [_mask_window(x, diff, lv) for x, lv in zip(st["sp"], left_valid[k])]
        st["spm"] = [x.astype(BF16) for x in spm]
        for u, (g, s) in enumerate(units):
            decay = jnp.sum(spm[u], axis=1, keepdims=True)
            decay_ref[s, :, g:g + 1] = decay
            unfinished = jnp.logical_and(jnp.min(decay) < DECAY_LIMIT, qi * n_sub + s >= 2)
            need_ref[k * per_group + u] = unfinished.astype(jnp.int32)
            need_ref[any_need] = need_ref[any_need] | unfinished.astype(jnp.int32)

    def stage_later(k):
        st = state[k]
        stacked = _dot(jnp.concatenate(st["spm"], axis=0), later_win)
        later = [stacked[u * 2 * sub:(u + 1) * 2 * sub] for u in range(per_group)]
        st["a"] = [_mask_window(_weights(*zsl), diff, lv).astype(BF16)
                   for zsl, lv in zip(zip(st["z"], st["sp"], later), left_valid[k])]

    def stage_values(k):
        for a, (g, s) in zip(state[k]["a"], groups[k]):
            pv = _apply(a, window(vp_ref, vo_ref, g, s))
            acc_ref[s * sub:(s + 1) * sub, lanes(g)] = pv
            out_ref[s * sub:(s + 1) * sub, lanes(g)] = pv.astype(out_ref.dtype)

    def thunk(t):
        for stage, k in ((stage_values, t - 2), (stage_later, t - 1), (stage_scores, t)):
            if 0 <= k < n_groups:
                stage(k)

    return [functools.partial(thunk, t) for t in range(n_groups + 2)]


def _attention_sweeps(q_ref, kp_ref, ko_ref, vp_ref, vo_ref, k_any, v_any, out_ref, acc_ref,
                      decay_ref, need_ref, kbuf, vbuf, sems, b, qi):
    sub = ATTN_SUB
    n_sub = q_ref.shape[1] // sub
    later_sub = _later_matrix(sub)

    for g in range(SB_WIDTH // LANES):
        lanes = slice(g * LANES, (g + 1) * LANES)

        def fetch(first_key):
            copies = [pltpu.make_async_copy(src.at[b, pl.ds(first_key, sub), lanes], dst, sems.at[i])
                      for i, (src, dst) in enumerate(((k_any, kbuf), (v_any, vbuf)))]
            for c in copies:
                c.start()
            for c in copies:
                c.wait()

        def per_sub(s, carry):
            @pl.when(need_ref[g * n_sub + s] != 0)
            def _():
                tile = qi * n_sub + s
                rows = pl.ds(pl.multiple_of(s * sub, sub), sub)
                qs = _stack_heads(q_ref[0, rows, lanes])

                def block(kb, vb, acc, decay):
                    z = _scores(qs, kb)
                    sp = _softplus2(z)
                    a = _weights(z, sp, _later_sum(sp, later_sub) + decay)
                    return acc + _apply(a, vb), decay + jnp.sum(sp, axis=1, keepdims=True)

                def resident(prev_ref, own_ref):
                    in_own = pl.ds(pl.multiple_of(jnp.maximum(s - 2, 0) * sub, sub), sub)
                    in_prev = pl.ds(pl.multiple_of(jnp.minimum(s, 1) * sub, sub), sub)
                    return lax.cond(s >= 2, lambda: own_ref[0, in_own, lanes],
                                    lambda: prev_ref[0, in_prev, lanes])

                acc, decay = block(resident(kp_ref, ko_ref), resident(vp_ref, vo_ref),
                                   acc_ref[rows, lanes], decay_ref[s, :, g:g + 1])

                def cond(st):
                    return jnp.logical_and(st[0] >= 0, st[1])

                def body(st):
                    j, _, acc, decay = st
                    fetch(pl.multiple_of(j * sub, sub))
                    acc, decay = block(kbuf[...], vbuf[...], acc, decay)
                    return j - 1, jnp.min(decay) < DECAY_LIMIT, acc, decay

                st = lax.while_loop(cond, body,
                                    (tile - 3, jnp.min(decay) < DECAY_LIMIT, acc, decay))
                out_ref[rows, lanes] = st[2].astype(out_ref.dtype)

            return carry

        lax.fori_loop(0, n_sub, per_sub, 0)


def _post_kernel(h_ref, gm_ref, p_ref, q_ref, kp_ref, ko_ref, vp_ref, vo_ref, k_any, v_any,
                 w_mo_ref, n2_ref, w_in_ref, w_out_ref, np_ref, w_gate_ref, w_proj_ref, nf_ref,
                 o_ref, act_ref, sb_ref, acc_ref, decay_ref, need_ref, kbuf, vbuf, sems,
                 *, final, tiles_per_seq):
    j = pl.program_id(0)
    n_tiles = pl.num_programs(0) - 1
    cur = j % 2

    @pl.when(j == 0)
    def _():
        sb_ref[1] = jnp.zeros(sb_ref.shape[1:], sb_ref.dtype)

    tile = jnp.minimum(j, n_tiles - 1)
    b, qi = tile // tiles_per_seq, tile % tiles_per_seq
    attention = _attention_windows(q_ref, kp_ref, ko_ref, vp_ref, vo_ref, sb_ref.at[cur],
                                   acc_ref, decay_ref, need_ref, qi, ATTN_GROUPS)

    mixed = jnp.concatenate([gm_ref[...], sb_ref[1 - cur]], axis=1)
    h = h_ref[...] + _dot(mixed, w_mo_ref[...])
    y = _rms(h, n2_ref[...]).astype(BF16)
    h = h + 0.5 * _half_swiglu(y, w_in_ref, w_out_ref, act_ref, attention[:-2])
    attention[-2]()
    gate = jax.nn.sigmoid(_dot(_rms(h, np_ref[...]).astype(BF16), w_gate_ref[...]))
    attention[-1]()
    h = h + gate * _dot(p_ref[...].astype(BF16), w_proj_ref[...])
    o_ref[...] = _rms(h, nf_ref[...]) if final else h

    @pl.when(need_ref[need_ref.shape[0] - 1] != 0)
    def _():
        _attention_sweeps(q_ref, kp_ref, ko_ref, vp_ref, vo_ref, k_any, v_any, sb_ref.at[cur],
                          acc_ref, decay_ref, need_ref, kbuf, vbuf, sems, b, qi)


def _post_call(h1, gm, p2, q, k, v, w_mo, n2, w_in, w_out, np_, w_gate, w_proj, nf, final):
    tokens = h1.shape[0]
    batch, seq, _ = q.shape
    tm = TOKEN_TILE
    n_tiles = tokens // tm
    tiles_per_seq = seq // tm
    sub_per_tile = tm // ATTN_SUB
    dense = lambda width: pl.BlockSpec((tm, width), lambda j: (jnp.maximum(j - 1, 0), 0))

    def own(j):
        t = jnp.minimum(j, n_tiles - 1)
        return t // tiles_per_seq, t % tiles_per_seq, 0

    def prev(j):
        b, qi, _ = own(j)
        return b, jnp.maximum(qi * (sub_per_tile // 2) - 1, 0), 0

    own_spec = pl.BlockSpec((1, tm, SB_WIDTH), own)
    prev_spec = pl.BlockSpec((1, 2 * ATTN_SUB, SB_WIDTH), prev)
    hbm = pl.BlockSpec(memory_space=pl.ANY)
    return pl.pallas_call(
        functools.partial(_post_kernel, final=final, tiles_per_seq=tiles_per_seq),
        grid=(n_tiles + 1,),
        in_specs=[
            dense(D_MODEL), dense(GM_WIDTH), dense(PLE_DIM),
            own_spec, prev_spec, own_spec, prev_spec, own_spec, hbm, hbm,
            _resident(w_mo.shape),
            _resident((1, D_MODEL)),
            _resident(w_in.shape),
            _resident(w_out.shape),
            _resident((1, D_MODEL)),
            _resident(w_gate.shape),
            _resident(w_proj.shape),
            _resident((1, D_MODEL)),
        ],
        out_specs=dense(D_MODEL),
        out_shape=jax.ShapeDtypeStruct((tokens, D_MODEL), F32),
        scratch_shapes=[
            pltpu.VMEM((tm, D_FF), BF16),
            pltpu.VMEM((2, tm, SB_WIDTH), BF16),
            pltpu.VMEM((tm, SB_WIDTH), F32),
            pltpu.VMEM((sub_per_tile, 2 * ATTN_SUB, LANES), F32),
            pltpu.SMEM((SB_WIDTH // LANES * sub_per_tile + 1,), jnp.int32),
            pltpu.VMEM((ATTN_SUB, LANES), BF16),
            pltpu.VMEM((ATTN_SUB, LANES), BF16),
            pltpu.SemaphoreType.DMA((2,)),
        ],
        compiler_params=pltpu.CompilerParams(
            dimension_semantics=("arbitrary",), vmem_limit_bytes=VMEM_LIMIT),
        name="post_attn_mix_out_ffn_ple",
    )(h1, gm, p2, q, k, k, v, v, k, v, w_mo, n2, w_in, w_out, np_, w_gate, w_proj, nf)


def kernel(x, p, ffn1_norm, ffn1_w_in, ffn1_w_out, mix_norm, w_mix_in, gmlp_v_norm, gmlp_w_s, gmlp_b, w_mix_out, ffn2_norm, ffn2_w_in, ffn2_w_out, ple_norm, ple_w_gate, ple_w_proj, final_norm):
    batch, seq, _ = x.shape
    depth = p.shape[0]
    tokens = batch * seq
    vec = lambda g: g.reshape(1, -1).astype(F32)

    h = x.reshape(tokens, D_MODEL)
    for i in range(depth):
        bs_full = jnp.repeat(gmlp_b[i].T, GM_HEAD_DIM, axis=1)
        h1, gm, q, k, v, w_mo, w_in2, w_out2, w_gate, w_proj = _pre_call(
            h, vec(ffn1_norm[i]), ffn1_w_in[i].astype(BF16), ffn1_w_out[i].astype(BF16),
            vec(mix_norm[i]), w_mix_in[i].astype(BF16), vec(gmlp_v_norm[i]),
            gmlp_w_s[i], bs_full,
            [w_mix_out, ffn2_w_in, ffn2_w_out, ple_w_gate, ple_w_proj], layer=i)
        shape3 = (batch, seq, SB_WIDTH)
        h = _post_call(
            h1, gm, p[i].reshape(tokens, PLE_DIM),
            q.reshape(shape3), k.reshape(shape3), v.reshape(shape3),
            w_mo, vec(ffn2_norm[i]), w_in2, w_out2, vec(ple_norm[i]), w_gate, w_proj,
            vec(final_norm), final=(i == depth - 1))
    return h.reshape(batch, seq, D_MODEL)
```

```python
import functools

import jax
import jax.numpy as jnp
from jax import lax
from jax.experimental import pallas as pl
from jax.experimental.pallas import tpu as pltpu

D_MODEL = 1024
PLE_DIM = 256
D_FF = 2816
CHUNK = 128
GM_HEADS = 4
GM_HEAD_DIM = 128
GM_WIDTH = GM_HEADS * GM_HEAD_DIM
SB_HEADS = 8
SB_HEAD_DIM = 64
SB_WIDTH = SB_HEADS * SB_HEAD_DIM
EPS = 1e-6

LANES = 128
FF_CHUNK = 256
TOKEN_TILE = 512
CAST_ROWS = 16
STAGE_CHUNKS = 16
ATTN_SUB = 128
ATTN_GROUPS = 8
LOG2E = 1.4426950408889634
DECAY_LIMIT = 127.0
VMEM_LIMIT = 56 * 1024 * 1024

BF16 = jnp.bfloat16
F32 = jnp.float32


def _dot(a, b):
    return jnp.dot(a, b, preferred_element_type=F32)


def _rms(x, g):
    return x * lax.rsqrt(jnp.mean(x * x, axis=-1, keepdims=True) + EPS) * g


def _half_swiglu(y, w_in_ref, w_out_ref, act_ref, side_work=()):
    for c in range(D_FF // FF_CHUNK):
        if c < len(side_work):
            side_work[c]()
        lo = c * FF_CHUNK
        gate = _dot(y, w_in_ref[:, lo:lo + FF_CHUNK])
        up = _dot(y, w_in_ref[:, D_FF + lo:D_FF + lo + FF_CHUNK])
        act_ref[:, lo:lo + FF_CHUNK] = (gate * jax.nn.sigmoid(gate) * up).astype(BF16)
    return _dot(act_ref[...], w_out_ref[...])


def _gelu(x):
    return 0.5 * x * (1.0 + lax.erf(x * (2.0 ** -0.5)))


def _stage_weight(src, layer, dst, stage, sems):
    rows = stage.shape[1]
    n_chunks = dst.shape[0] // rows

    def chunk(c):
        slot = c % 2
        return pltpu.make_async_copy(
            src.at[layer, pl.ds(pl.multiple_of(c * rows, rows), rows)], stage.at[slot],
            sems.at[slot])

    chunk(0).start()

    def body(c, carry):
        @pl.when(c + 1 < n_chunks)
        def _():
            chunk(c + 1).start()

        chunk(c).wait()
        dst[pl.ds(pl.multiple_of(c * rows, rows), rows), :] = stage[c % 2].astype(BF16)
        return carry

    lax.fori_loop(0, n_chunks, body, 0)


def _pre_kernel(x_ref, n1_ref, w_in_hbm, w_out_hbm, nm_ref, w_mix_hbm, vg_ref, ws_ref, bs_ref,
                *refs, layer):
    n_cast = (len(refs) - 13) // 2
    cast_in, refs = refs[:n_cast], refs[n_cast:]
    h_ref, gm_ref, q_ref, k_ref, v_ref = refs[:5]
    cast_out = refs[5:5 + n_cast]
    act_ref, w_in_ref, w_out_ref, w_mix_ref, st_in, st_out, st_mix, sems = refs[5 + n_cast:]

    @pl.when(pl.program_id(0) == 0)
    def _():
        _stage_weight(w_in_hbm, layer, w_in_ref, st_in, sems.at[0])
        _stage_weight(w_out_hbm, layer, w_out_ref, st_out, sems.at[1])
        _stage_weight(w_mix_hbm, layer, w_mix_ref, st_mix, sems.at[2])

    for src, dst in zip(cast_in, cast_out):
        dst[...] = src[...].astype(BF16)

    x = x_ref[...]
    y = _rms(x, n1_ref[...]).astype(BF16)
    h = x + 0.5 * _half_swiglu(y, w_in_ref, w_out_ref, act_ref)
    h_ref[...] = h

    n = _rms(h, nm_ref[...]).astype(BF16)
    zu = _dot(n, w_mix_ref[:, 0:GM_WIDTH])
    zv = _dot(n, w_mix_ref[:, GM_WIDTH:2 * GM_WIDTH])

    base = 2 * GM_WIDTH
    q_ref[...] = (_dot(n, w_mix_ref[:, base:base + SB_WIDTH])
                  * (SB_HEAD_DIM ** -0.5 * LOG2E)).astype(BF16)
    k_ref[...] = _dot(n, w_mix_ref[:, base + SB_WIDTH:base + 2 * SB_WIDTH]).astype(BF16)
    v_ref[...] = _dot(n, w_mix_ref[:, base + 2 * SB_WIDTH:base + 3 * SB_WIDTH]).astype(BF16)

    u = _gelu(zu)
    v = _gelu(zv)
    vn = _rms(v, vg_ref[...]).astype(BF16)

    t_idx = lax.broadcasted_iota(jnp.int32, (CHUNK, CHUNK), 0)
    s_idx = lax.broadcasted_iota(jnp.int32, (CHUNK, CHUNK), 1)
    keep = s_idx <= t_idx
    for hd in range(GM_HEADS):
        w = jnp.where(keep, ws_ref[hd], 0.0).astype(BF16)
        cols = slice(hd * GM_HEAD_DIM, (hd + 1) * GM_HEAD_DIM)
        chunks = [slice(c * CHUNK, (c + 1) * CHUNK) for c in range(x.shape[0] // CHUNK)]
        sv = _dot(w, jnp.concatenate([vn[rows, cols] for rows in chunks], axis=1))
        for rows in chunks:
            gm_ref[rows, cols] = (u[rows, cols] * (sv[:, rows] + bs_ref[:, cols])).astype(BF16)


def _resident(shape):
    return pl.BlockSpec(shape, lambda *_: (0,) * len(shape), pipeline_mode=pl.Buffered(1))


def _cast_blocks(rows, steps):
    tiles = rows // CAST_ROWS
    return max(d for d in range(1, min(steps, tiles) + 1) if tiles % d == 0)


def _stage_rows(rows):
    assert rows % (STAGE_CHUNKS * CAST_ROWS) == 0
    return rows // STAGE_CHUNKS


def _pre_call(x2, n1, w_in, w_out, nm, w_mix, vg, ws, bs_full, next_weights, layer):
    tokens = x2.shape[0]
    tm = TOKEN_TILE
    steps = tokens // tm
    row = lambda width: pl.BlockSpec((tm, width), lambda i: (i, 0))
    hbm = pl.BlockSpec(memory_space=pl.ANY)
    cast_in, cast_out = [], []
    for w in next_weights:
        _, rows, cols = w.shape
        blocks = _cast_blocks(rows, steps)
        cast_in.append(pl.BlockSpec(
            (None, rows // blocks, cols),
            lambda i, last=blocks - 1: (layer, jnp.minimum(i, last), 0)))
        cast_out.append(pl.BlockSpec(
            (rows // blocks, cols), lambda i, last=blocks - 1: (jnp.minimum(i, last), 0)))
    own = (w_in, w_out, w_mix)
    return pl.pallas_call(
        functools.partial(_pre_kernel, layer=layer),
        grid=(steps,),
        in_specs=[
            row(D_MODEL),
            _resident((1, D_MODEL)),
            hbm,
            hbm,
            _resident((1, D_MODEL)),
            hbm,
            _resident((1, GM_WIDTH)),
            _resident(ws.shape),
            _resident(bs_full.shape),
        ] + cast_in,
        out_specs=[row(D_MODEL), row(GM_WIDTH), row(SB_WIDTH), row(SB_WIDTH), row(SB_WIDTH)]
        + cast_out,
        out_shape=[
            jax.ShapeDtypeStruct((tokens, D_MODEL), F32),
            jax.ShapeDtypeStruct((tokens, GM_WIDTH), BF16),
            jax.ShapeDtypeStruct((tokens, SB_WIDTH), BF16),
            jax.ShapeDtypeStruct((tokens, SB_WIDTH), BF16),
            jax.ShapeDtypeStruct((tokens, SB_WIDTH), BF16),
        ] + [jax.ShapeDtypeStruct(w.shape[1:], BF16) for w in next_weights],
        scratch_shapes=[pltpu.VMEM((tm, D_FF), BF16)]
        + [pltpu.VMEM(w.shape[1:], BF16) for w in own]
        + [pltpu.VMEM((2, _stage_rows(w.shape[1]), w.shape[2]), F32) for w in own]
        + [pltpu.SemaphoreType.DMA((len(own), 2))],
        compiler_params=pltpu.CompilerParams(
            dimension_semantics=("arbitrary",), vmem_limit_bytes=VMEM_LIMIT),
        name="pre_ffn_mix_in",
    )(x2, n1, w_in, w_out, nm, w_mix, vg, ws, bs_full, *next_weights)


def _later_matrix(n):
    j = lax.broadcasted_iota(jnp.int32, (n, n), 0)
    s = lax.broadcasted_iota(jnp.int32, (n, n), 1)
    return (j > s).astype(BF16)


def _stack_heads(x):
    first = lax.broadcasted_iota(jnp.int32, x.shape, 1) < SB_HEAD_DIM
    zero = jnp.zeros_like(x)
    return jnp.concatenate([jnp.where(first, x, zero), jnp.where(first, zero, x)], axis=0)


def _scores(qs, kb):
    return lax.dot_general(qs, kb, (((1,), (1,)), ((), ())), preferred_element_type=F32)


def _softplus2(z):
    return jnp.maximum(z, 0.0) + jnp.log2(1.0 + jnp.exp2(-jnp.abs(z)))


def _mask_window(x, diff, left_valid):
    left, right = x[:, :ATTN_SUB], x[:, ATTN_SUB:]
    if left_valid is not True:
        left = jnp.where(left_valid, left, 0.0)
    right = jnp.where(diff < 0, right, 0.0)
    return jnp.concatenate([left, right], axis=1)


def _later_sum(sp, later):
    return _dot(sp.astype(BF16), later)


def _weights(z, sp, decay):
    return jnp.exp2((z - sp) - decay)


def _apply(a, vb):
    pv = _dot(a.astype(BF16), vb)
    first = lax.broadcasted_iota(jnp.int32, (ATTN_SUB, LANES), 1) < SB_HEAD_DIM
    return jnp.where(first, pv[:ATTN_SUB], pv[ATTN_SUB:])


def _window_consts():
    sub = ATTN_SUB
    diff = (lax.broadcasted_iota(jnp.int32, (2 * sub, sub), 1)
            - lax.broadcasted_iota(jnp.int32, (2 * sub, sub), 0) % sub)
    return _later_matrix(2 * sub), diff


def _attention_windows(q_ref, kp_ref, ko_ref, vp_ref, vo_ref, out_ref, acc_ref, decay_ref,
                       need_ref, qi, n_groups):
    sub = ATTN_SUB
    n_sub = q_ref.shape[1] // sub
    all_units = [(g, s) for g in range(SB_WIDTH // LANES) for s in range(n_sub)]
    per_group = len(all_units) // n_groups
    lanes = lambda g: slice(g * LANES, (g + 1) * LANES)

    def window(prev_ref, own_ref, g, s):
        if s == 0:
            return jnp.concatenate([prev_ref[0, sub:2 * sub, lanes(g)],
                                    own_ref[0, 0:sub, lanes(g)]], axis=0)
        return own_ref[0, (s - 1) * sub:(s + 1) * sub, lanes(g)]

    later_win, diff = _window_consts()
    any_need = len(all_units)
    need_ref[any_need] = 0
    groups = [all_units[k * per_group:(k + 1) * per_group] for k in range(n_groups)]
    left_valid = [[qi > 0 if s == 0 else True for _, s in units] for units in groups]
    state = [{} for _ in groups]

    def stage_scores(k):
        units, st = groups[k], state[k]
        st["z"] = [_scores(_stack_heads(q_ref[0, s * sub:(s + 1) * sub, lanes(g)]),
                           window(kp_ref, ko_ref, g, s)) for g, s in units]
        st["sp"] = [_softplus2(x) for x in st["z"]]
        spm = [_mask_window(x, diff, lv) for x, lv in zip(st["sp"], left_valid[k])]
        st["spm"] = [x.astype(BF16) for x in spm]
        for u, (g, s) in enumerate(units):
            decay = jnp.sum(spm[u], axis=1, keepdims=True)
            decay_ref[s, :, g:g + 1] = decay
            unfinished = jnp.logical_and(jnp.min(decay) < DECAY_LIMIT, qi * n_sub + s >= 2)
            need_ref[k * per_group + u] = unfinished.astype(jnp.int32)
            need_ref[any_need] = need_ref[any_need] | unfinished.astype(jnp.int32)

    def stage_later(k):
        st = state[k]
        stacked = _dot(jnp.concatenate(st["spm"], axis=0), later_win)
        later = [stacked[u * 2 * sub:(u + 1) * 2 * sub] for u in range(per_group)]
        st["a"] = [_mask_window(_weights(*zsl), diff, lv).astype(BF16)
                   for zsl, lv in zip(zip(st["z"], st["sp"], later), left_valid[k])]

    def stage_values(k):
        for a, (g, s) in zip(state[k]["a"], groups[k]):
            pv = _apply(a, window(vp_ref, vo_ref, g, s))
            acc_ref[s * sub:(s + 1) * sub, lanes(g)] = pv
            out_ref[s * sub:(s + 1) * sub, lanes(g)] = pv.astype(out_ref.dtype)

    def thunk(t):
        for stage, k in ((stage_values, t - 2), (stage_later, t - 1), (stage_scores, t)):
            if 0 <= k < n_groups:
                stage(k)

    return [functools.partial(thunk, t) for t in range(n_groups + 2)]


def _attention_sweeps(q_ref, kp_ref, ko_ref, vp_ref, vo_ref, k_any, v_any, out_ref, acc_ref,
                      decay_ref, need_ref, kbuf, vbuf, sems, b, qi):
    sub = ATTN_SUB
    n_sub = q_ref.shape[1] // sub
    later_sub = _later_matrix(sub)

    for g in range(SB_WIDTH // LANES):
        lanes = slice(g * LANES, (g + 1) * LANES)

        def fetch(first_key):
            copies = [pltpu.make_async_copy(src.at[b, pl.ds(first_key, sub), lanes], dst, sems.at[i])
                      for i, (src, dst) in enumerate(((k_any, kbuf), (v_any, vbuf)))]
            for c in copies:
                c.start()
            for c in copies:
                c.wait()

        def per_sub(s, carry):
            @pl.when(need_ref[g * n_sub + s] != 0)
            def _():
                tile = qi * n_sub + s
                rows = pl.ds(pl.multiple_of(s * sub, sub), sub)
                qs = _stack_heads(q_ref[0, rows, lanes])

                def block(kb, vb, acc, decay):
                    z = _scores(qs, kb)
                    sp = _softplus2(z)
                    a = _weights(z, sp, _later_sum(sp, later_sub) + decay)
                    return acc + _apply(a, vb), decay + jnp.sum(sp, axis=1, keepdims=True)

                def resident(prev_ref, own_ref):
                    in_own = pl.ds(pl.multiple_of(jnp.maximum(s - 2, 0) * sub, sub), sub)
                    in_prev = pl.ds(pl.multiple_of(jnp.minimum(s, 1) * sub, sub), sub)
                    return lax.cond(s >= 2, lambda: own_ref[0, in_own, lanes],
                                    lambda: prev_ref[0, in_prev, lanes])

                acc, decay = block(resident(kp_ref, ko_ref), resident(vp_ref, vo_ref),
                                   acc_ref[rows, lanes], decay_ref[s, :, g:g + 1])

                def cond(st):
                    return jnp.logical_and(st[0] >= 0, st[1])

                def body(st):
                    j, _, acc, decay = st
                    fetch(pl.multiple_of(j * sub, sub))
                    acc, decay = block(kbuf[...], vbuf[...], acc, decay)
                    return j - 1, jnp.min(decay) < DECAY_LIMIT, acc, decay

                st = lax.while_loop(cond, body,
                                    (tile - 3, jnp.min(decay) < DECAY_LIMIT, acc, decay))
                out_ref[rows, lanes] = st[2].astype(out_ref.dtype)

            return carry

        lax.fori_loop(0, n_sub, per_sub, 0)


def _post_kernel(h_ref, gm_ref, p_ref, q_ref, kp_ref, ko_ref, vp_ref, vo_ref, k_any, v_any,
                 w_mo_ref, n2_ref, w_in_ref, w_out_ref, np_ref, w_gate_ref, w_proj_ref, nf_ref,
                 o_ref, act_ref, sb_ref, acc_ref, decay_ref, need_ref, kbuf, vbuf, sems,
                 *, final, tiles_per_seq):
    j = pl.program_id(0)
    n_tiles = pl.num_programs(0) - 1
    cur = j % 2

    @pl.when(j == 0)
    def _():
        sb_ref[1] = jnp.zeros(sb_ref.shape[1:], sb_ref.dtype)

    tile = jnp.minimum(j, n_tiles - 1)
    b, qi = tile // tiles_per_seq, tile % tiles_per_seq
    attention = _attention_windows(q_ref, kp_ref, ko_ref, vp_ref, vo_ref, sb_ref.at[cur],
                                   acc_ref, decay_ref, need_ref, qi, ATTN_GROUPS)

    mixed = jnp.concatenate([gm_ref[...], sb_ref[1 - cur]], axis=1)
    h = h_ref[...] + _dot(mixed, w_mo_ref[...])
    y = _rms(h, n2_ref[...]).astype(BF16)
    h = h + 0.5 * _half_swiglu(y, w_in_ref, w_out_ref, act_ref, attention[:-2])
    attention[-2]()
    gate = jax.nn.sigmoid(_dot(_rms(h, np_ref[...]).astype(BF16), w_gate_ref[...]))
    attention[-1]()
    h = h + gate * _dot(p_ref[...].astype(BF16), w_proj_ref[...])
    o_ref[...] = _rms(h, nf_ref[...]) if final else h

    @pl.when(need_ref[need_ref.shape[0] - 1] != 0)
    def _():
        _attention_sweeps(q_ref, kp_ref, ko_ref, vp_ref, vo_ref, k_any, v_any, sb_ref.at[cur],
                          acc_ref, decay_ref, need_ref, kbuf, vbuf, sems, b, qi)


def _post_call(h1, gm, p2, q, k, v, w_mo, n2, w_in, w_out, np_, w_gate, w_proj, nf, final):
    tokens = h1.shape[0]
    batch, seq, _ = q.shape
    tm = TOKEN_TILE
    n_tiles = tokens // tm
    tiles_per_seq = seq // tm
    sub_per_tile = tm // ATTN_SUB
    dense = lambda width: pl.BlockSpec((tm, width), lambda j: (jnp.maximum(j - 1, 0), 0))

    def own(j):
        t = jnp.minimum(j, n_tiles - 1)
        return t // tiles_per_seq, t % tiles_per_seq, 0

    def prev(j):
        b, qi, _ = own(j)
        return b, jnp.maximum(qi * (sub_per_tile // 2) - 1, 0), 0

    own_spec = pl.BlockSpec((1, tm, SB_WIDTH), own)
    prev_spec = pl.BlockSpec((1, 2 * ATTN_SUB, SB_WIDTH), prev)
    hbm = pl.BlockSpec(memory_space=pl.ANY)
    return pl.pallas_call(
        functools.partial(_post_kernel, final=final, tiles_per_seq=tiles_per_seq),
        grid=(n_tiles + 1,),
        in_specs=[
            dense(D_MODEL), dense(GM_WIDTH), dense(PLE_DIM),
            own_spec, prev_spec, own_spec, prev_spec, own_spec, hbm, hbm,
            _resident(w_mo.shape),
            _resident((1, D_MODEL)),
            _resident(w_in.shape),
            _resident(w_out.shape),
            _resident((1, D_MODEL)),
            _resident(w_gate.shape),
            _resident(w_proj.shape),
            _resident((1, D_MODEL)),
        ],
        out_specs=dense(D_MODEL),
        out_shape=jax.ShapeDtypeStruct((tokens, D_MODEL), F32),
        scratch_shapes=[
            pltpu.VMEM((tm, D_FF), BF16),
            pltpu.VMEM((2, tm, SB_WIDTH), BF16),
            pltpu.VMEM((tm, SB_WIDTH), F32),
            pltpu.VMEM((sub_per_tile, 2 * ATTN_SUB, LANES), F32),
            pltpu.SMEM((SB_WIDTH // LANES * sub_per_tile + 1,), jnp.int32),
            pltpu.VMEM((ATTN_SUB, LANES), BF16),
            pltpu.VMEM((ATTN_SUB, LANES), BF16),
            pltpu.SemaphoreType.DMA((2,)),
        ],
        compiler_params=pltpu.CompilerParams(
            dimension_semantics=("arbitrary",), vmem_limit_bytes=VMEM_LIMIT),
        name="post_attn_mix_out_ffn_ple",
    )(h1, gm, p2, q, k, k, v, v, k, v, w_mo, n2, w_in, w_out, np_, w_gate, w_proj, nf)


def kernel(x, p, ffn1_norm, ffn1_w_in, ffn1_w_out, mix_norm, w_mix_in, gmlp_v_norm, gmlp_w_s, gmlp_b, w_mix_out, ffn2_norm, ffn2_w_in, ffn2_w_out, ple_norm, ple_w_gate, ple_w_proj, final_norm):
    batch, seq, _ = x.shape
    depth = p.shape[0]
    tokens = batch * seq
    vec = lambda g: g.reshape(1, -1).astype(F32)

    h = x.reshape(tokens, D_MODEL)
    for i in range(depth):
        bs_full = jnp.repeat(gmlp_b[i].T, GM_HEAD_DIM, axis=1)
        h1, gm, q, k, v, w_mo, w_in2, w_out2, w_gate, w_proj = _pre_call(
            h, vec(ffn1_norm[i]), ffn1_w_in, ffn1_w_out, vec(mix_norm[i]), w_mix_in,
            vec(gmlp_v_norm[i]), gmlp_w_s[i], bs_full,
            [w_mix_out, ffn2_w_in, ffn2_w_out, ple_w_gate, ple_w_proj], layer=i)
        shape3 = (batch, seq, SB_WIDTH)
        h = _post_call(
            h1, gm, p[i].reshape(tokens, PLE_DIM),
            q.reshape(shape3), k.reshape(shape3), v.reshape(shape3),
            w_mo, vec(ffn2_norm[i]), w_in2, w_out2, vec(ple_norm[i]), w_gate, w_proj,
            vec(final_norm), final=(i == depth - 1))
    return h.reshape(batch, seq, D_MODEL)
```

```python
import functools

import jax
import jax.numpy as jnp
from jax import lax
from jax.experimental import pallas as pl
from jax.experimental.pallas import tpu as pltpu

D_MODEL = 1024
PLE_DIM = 256
D_FF = 2816
CHUNK = 128
GM_HEADS = 4
GM_HEAD_DIM = 128
GM_WIDTH = GM_HEADS * GM_HEAD_DIM
SB_HEADS = 8
SB_HEAD_DIM = 64
SB_WIDTH = SB_HEADS * SB_HEAD_DIM
EPS = 1e-6

LANES = 128
FF_CHUNK = 256
TOKEN_TILE = 512
CAST_ROWS = 16
STAGE_CHUNKS = 16
ATTN_SUB = 128
ATTN_GROUPS = 8
LOG2E = 1.4426950408889634
DECAY_LIMIT = 127.0
VMEM_LIMIT = 56 * 1024 * 1024

BF16 = jnp.bfloat16
F32 = jnp.float32


def _dot(a, b):
    return jnp.dot(a, b, preferred_element_type=F32)


def _rms(x, g):
    return x * lax.rsqrt(jnp.mean(x * x, axis=-1, keepdims=True) + EPS) * g


def _half_swiglu(y, w_in_ref, w_out_ref, act_ref, side_work=()):
    for c in range(D_FF // FF_CHUNK):
        if c < len(side_work):
            side_work[c]()
        lo = c * FF_CHUNK
        gate = _dot(y, w_in_ref[:, lo:lo + FF_CHUNK])
        up = _dot(y, w_in_ref[:, D_FF + lo:D_FF + lo + FF_CHUNK])
        act_ref[:, lo:lo + FF_CHUNK] = (gate * jax.nn.sigmoid(gate) * up).astype(BF16)
    return _dot(act_ref[...], w_out_ref[...])


def _gelu(x):
    return 0.5 * x * (1.0 + lax.erf(x * (2.0 ** -0.5)))


def _stage_weights(weights, layer):
    def chunk(w, c):
        src, _, stage, sems = w
        rows = stage.shape[1]
        slot = c % 2
        return pltpu.make_async_copy(
            src.at[layer, pl.ds(pl.multiple_of(c * rows, rows), rows)], stage.at[slot],
            sems.at[slot])

    for w in weights:
        chunk(w, 0).start()

    def body(c, carry):
        @pl.when(c + 1 < STAGE_CHUNKS)
        def _():
            for w in weights:
                chunk(w, c + 1).start()

        for w in weights:
            _, dst, stage, _ = w
            rows = stage.shape[1]
            chunk(w, c).wait()
            dst[pl.ds(pl.multiple_of(c * rows, rows), rows), :] = stage[c % 2].astype(BF16)
        return carry

    lax.fori_loop(0, STAGE_CHUNKS, body, 0)


def _pre_kernel(x_ref, n1_ref, w_in_hbm, w_out_hbm, nm_ref, w_mix_hbm, vg_ref, ws_ref, bs_ref,
                *refs, layer):
    n_cast = (len(refs) - 15) // 2
    cast_in, refs = refs[:n_cast], refs[n_cast:]
    h_ref, gm_ref, q_ref, k_ref, v_ref = refs[:5]
    cast_out = refs[5:5 + n_cast]
    act_ref, w_in_ref, w_out_ref, w_mix_ref = refs[5 + n_cast:9 + n_cast]
    stages, sems = refs[9 + n_cast:12 + n_cast], refs[12 + n_cast:]

    @pl.when(pl.program_id(0) == 0)
    def _():
        _stage_weights(list(zip((w_in_hbm, w_out_hbm, w_mix_hbm),
                                (w_in_ref, w_out_ref, w_mix_ref), stages, sems)), layer)

    for src, dst in zip(cast_in, cast_out):
        dst[...] = src[...].astype(BF16)

    x = x_ref[...]
    y = _rms(x, n1_ref[...]).astype(BF16)
    h = x + 0.5 * _half_swiglu(y, w_in_ref, w_out_ref, act_ref)
    h_ref[...] = h

    n = _rms(h, nm_ref[...]).astype(BF16)
    zu = _dot(n, w_mix_ref[:, 0:GM_WIDTH])
    zv = _dot(n, w_mix_ref[:, GM_WIDTH:2 * GM_WIDTH])

    base = 2 * GM_WIDTH
    q_ref[...] = (_dot(n, w_mix_ref[:, base:base + SB_WIDTH])
                  * (SB_HEAD_DIM ** -0.5 * LOG2E)).astype(BF16)
    k_ref[...] = _dot(n, w_mix_ref[:, base + SB_WIDTH:base + 2 * SB_WIDTH]).astype(BF16)
    v_ref[...] = _dot(n, w_mix_ref[:, base + 2 * SB_WIDTH:base + 3 * SB_WIDTH]).astype(BF16)

    u = _gelu(zu)
    v = _gelu(zv)
    vn = _rms(v, vg_ref[...]).astype(BF16)

    t_idx = lax.broadcasted_iota(jnp.int32, (CHUNK, CHUNK), 0)
    s_idx = lax.broadcasted_iota(jnp.int32, (CHUNK, CHUNK), 1)
    keep = s_idx <= t_idx
    for hd in range(GM_HEADS):
        w = jnp.where(keep, ws_ref[hd], 0.0).astype(BF16)
        cols = slice(hd * GM_HEAD_DIM, (hd + 1) * GM_HEAD_DIM)
        chunks = [slice(c * CHUNK, (c + 1) * CHUNK) for c in range(x.shape[0] // CHUNK)]
        sv = _dot(w, jnp.concatenate([vn[rows, cols] for rows in chunks], axis=1))
        for rows in chunks:
            gm_ref[rows, cols] = (u[rows, cols] * (sv[:, rows] + bs_ref[:, cols])).astype(BF16)


def _resident(shape):
    return pl.BlockSpec(shape, lambda *_: (0,) * len(shape), pipeline_mode=pl.Buffered(1))


def _cast_blocks(rows, steps):
    tiles = rows // CAST_ROWS
    return max(d for d in range(1, min(steps, tiles) + 1) if tiles % d == 0)


def _stage_rows(rows):
    assert rows % (STAGE_CHUNKS * CAST_ROWS) == 0
    return rows // STAGE_CHUNKS


def _pre_call(x2, n1, w_in, w_out, nm, w_mix, vg, ws, bs_full, next_weights, layer):
    tokens = x2.shape[0]
    tm = TOKEN_TILE
    steps = tokens // tm
    row = lambda width: pl.BlockSpec((tm, width), lambda i: (i, 0))
    hbm = pl.BlockSpec(memory_space=pl.ANY)
    cast_in, cast_out = [], []
    for w in next_weights:
        _, rows, cols = w.shape
        blocks = _cast_blocks(rows, steps)
        cast_in.append(pl.BlockSpec(
            (None, rows // blocks, cols),
            lambda i, last=blocks - 1: (layer, jnp.minimum(i, last), 0)))
        cast_out.append(pl.BlockSpec(
            (rows // blocks, cols), lambda i, last=blocks - 1: (jnp.minimum(i, last), 0)))
    own = (w_in, w_out, w_mix)
    return pl.pallas_call(
        functools.partial(_pre_kernel, layer=layer),
        grid=(steps,),
        in_specs=[
            row(D_MODEL),
            _resident((1, D_MODEL)),
            hbm,
            hbm,
            _resident((1, D_MODEL)),
            hbm,
            _resident((1, GM_WIDTH)),
            _resident(ws.shape),
            _resident(bs_full.shape),
        ] + cast_in,
        out_specs=[row(D_MODEL), row(GM_WIDTH), row(SB_WIDTH), row(SB_WIDTH), row(SB_WIDTH)]
        + cast_out,
        out_shape=[
            jax.ShapeDtypeStruct((tokens, D_MODEL), F32),
            jax.ShapeDtypeStruct((tokens, GM_WIDTH), BF16),
            jax.ShapeDtypeStruct((tokens, SB_WIDTH), BF16),
            jax.ShapeDtypeStruct((tokens, SB_WIDTH), BF16),
            jax.ShapeDtypeStruct((tokens, SB_WIDTH), BF16),
        ] + [jax.ShapeDtypeStruct(w.shape[1:], BF16) for w in next_weights],
        scratch_shapes=[pltpu.VMEM((tm, D_FF), BF16)]
        + [pltpu.VMEM(w.shape[1:], BF16) for w in own]
        + [pltpu.VMEM((2, _stage_rows(w.shape[1]), w.shape[2]), F32) for w in own]
        + [pltpu.SemaphoreType.DMA((2,)) for _ in own],
        compiler_params=pltpu.CompilerParams(
            dimension_semantics=("arbitrary",), vmem_limit_bytes=VMEM_LIMIT),
        name="pre_ffn_mix_in",
    )(x2, n1, w_in, w_out, nm, w_mix, vg, ws, bs_full, *next_weights)


def _later_matrix(n):
    j = lax.broadcasted_iota(jnp.int32, (n, n), 0)
    s = lax.broadcasted_iota(jnp.int32, (n, n), 1)
    return (j > s).astype(BF16)


def _stack_heads(x):
    first = lax.broadcasted_iota(jnp.int32, x.shape, 1) < SB_HEAD_DIM
    zero = jnp.zeros_like(x)
    return jnp.concatenate([jnp.where(first, x, zero), jnp.where(first, zero, x)], axis=0)


def _scores(qs, kb):
    return lax.dot_general(qs, kb, (((1,), (1,)), ((), ())), preferred_element_type=F32)


def _softplus2(z):
    return jnp.maximum(z, 0.0) + jnp.log2(1.0 + jnp.exp2(-jnp.abs(z)))


def _mask_window(x, diff, left_valid):
    left, right = x[:, :ATTN_SUB], x[:, ATTN_SUB:]
    if left_valid is not True:
        left = jnp.where(left_valid, left, 0.0)
    right = jnp.where(diff < 0, right, 0.0)
    return jnp.concatenate([left, right], axis=1)


def _later_sum(sp, later):
    return _dot(sp.astype(BF16), later)


def _weights(z, sp, decay):
    return jnp.exp2((z - sp) - decay)


def _apply(a, vb):
    pv = _dot(a.astype(BF16), vb)
    first = lax.broadcasted_iota(jnp.int32, (ATTN_SUB, LANES), 1) < SB_HEAD_DIM
    return jnp.where(first, pv[:ATTN_SUB], pv[ATTN_SUB:])


def _window_consts():
    sub = ATTN_SUB
    diff = (lax.broadcasted_iota(jnp.int32, (2 * sub, sub), 1)
            - lax.broadcasted_iota(jnp.int32, (2 * sub, sub), 0) % sub)
    return _later_matrix(2 * sub), diff


def _attention_windows(q_ref, kp_ref, ko_ref, vp_ref, vo_ref, out_ref, acc_ref, decay_ref,
                       need_ref, qi, n_groups):
    sub = ATTN_SUB
    n_sub = q_ref.shape[1] // sub
    all_units = [(g, s) for g in range(SB_WIDTH // LANES) for s in range(n_sub)]
    per_group = len(all_units) // n_groups
    lanes = lambda g: slice(g * LANES, (g + 1) * LANES)

    def window(prev_ref, own_ref, g, s):
        if s == 0:
            return jnp.concatenate([prev_ref[0, sub:2 * sub, lanes(g)],
                                    own_ref[0, 0:sub, lanes(g)]], axis=0)
        return own_ref[0, (s - 1) * sub:(s + 1) * sub, lanes(g)]

    later_win, diff = _window_consts()
    any_need = len(all_units)
    need_ref[any_need] = 0
    groups = [all_units[k * per_group:(k + 1) * per_group] for k in range(n_groups)]
    left_valid = [[qi > 0 if s == 0 else True for _, s in units] for units in groups]
    state = [{} for _ in groups]

    def stage_scores(k):
        units, st = groups[k], state[k]
        st["z"] = [_scores(_stack_heads(q_ref[0, s * sub:(s + 1) * sub, lanes(g)]),
                           window(kp_ref, ko_ref, g, s)) for g, s in units]
        st["sp"] = [_softplus2(x) for x in st["z"]]
        spm = [_mask_window(x, diff, lv) for x, lv in zip(st["sp"], left_valid[k])]
        st["spm"] = [x.astype(BF16) for x in spm]
        for u, (g, s) in enumerate(units):
            decay = jnp.sum(spm[u], axis=1, keepdims=True)
            decay_ref[s, :, g:g + 1] = decay
            unfinished = jnp.logical_and(jnp.min(decay) < DECAY_LIMIT, qi * n_sub + s >= 2)
            need_ref[k * per_group + u] = unfinished.astype(jnp.int32)
            need_ref[any_need] = need_ref[any_need] | unfinished.astype(jnp.int32)

    def stage_later(k):
        st = state[k]
        stacked = _dot(jnp.concatenate(st["spm"], axis=0), later_win)
        later = [stacked[u * 2 * sub:(u + 1) * 2 * sub] for u in range(per_group)]
        st["a"] = [_mask_window(_weights(*zsl), diff, lv).astype(BF16)
                   for zsl, lv in zip(zip(st["z"], st["sp"], later), left_valid[k])]

    def stage_values(k):
        for a, (g, s) in zip(state[k]["a"], groups[k]):
            pv = _apply(a, window(vp_ref, vo_ref, g, s))
            acc_ref[s * sub:(s + 1) * sub, lanes(g)] = pv
            out_ref[s * sub:(s + 1) * sub, lanes(g)] = pv.astype(out_ref.dtype)

    def thunk(t):
        for stage, k in ((stage_values, t - 2), (stage_later, t - 1), (stage_scores, t)):
            if 0 <= k < n_groups:
                stage(k)

    return [functools.partial(thunk, t) for t in range(n_groups + 2)]


def _attention_sweeps(q_ref, kp_ref, ko_ref, vp_ref, vo_ref, k_any, v_any, out_ref, acc_ref,
                      decay_ref, need_ref, kbuf, vbuf, sems, b, qi):
    sub = ATTN_SUB
    n_sub = q_ref.shape[1] // sub
    later_sub = _later_matrix(sub)

    for g in range(SB_WIDTH // LANES):
        lanes = slice(g * LANES, (g + 1) * LANES)

        def fetch(first_key):
            copies = [pltpu.make_async_copy(src.at[b, pl.ds(first_key, sub), lanes], dst, sems.at[i])
                      for i, (src, dst) in enumerate(((k_any, kbuf), (v_any, vbuf)))]
            for c in copies:
                c.start()
            for c in copies:
                c.wait()

        def per_sub(s, carry):
            @pl.when(need_ref[g * n_sub + s] != 0)
            def _():
                tile = qi * n_sub + s
                rows = pl.ds(pl.multiple_of(s * sub, sub), sub)
                qs = _stack_heads(q_ref[0, rows, lanes])

                def block(kb, vb, acc, decay):
                    z = _scores(qs, kb)
                    sp = _softplus2(z)
                    a = _weights(z, sp, _later_sum(sp, later_sub) + decay)
                    return acc + _apply(a, vb), decay + jnp.sum(sp, axis=1, keepdims=True)

                def resident(prev_ref, own_ref):
                    in_own = pl.ds(pl.multiple_of(jnp.maximum(s - 2, 0) * sub, sub), sub)
                    in_prev = pl.ds(pl.multiple_of(jnp.minimum(s, 1) * sub, sub), sub)
                    return lax.cond(s >= 2, lambda: own_ref[0, in_own, lanes],
                                    lambda: prev_ref[0, in_prev, lanes])

                acc, decay = block(resident(kp_ref, ko_ref), resident(vp_ref, vo_ref),
                                   acc_ref[rows, lanes], decay_ref[s, :, g:g + 1])

                def cond(st):
                    return jnp.logical_and(st[0] >= 0, st[1])

                def body(st):
                    j, _, acc, decay = st
                    fetch(pl.multiple_of(j * sub, sub))
                    acc, decay = block(kbuf[...], vbuf[...], acc, decay)
                    return j - 1, jnp.min(decay) < DECAY_LIMIT, acc, decay

                st = lax.while_loop(cond, body,
                                    (tile - 3, jnp.min(decay) < DECAY_LIMIT, acc, decay))
                out_ref[rows, lanes] = st[2].astype(out_ref.dtype)

            return carry

        lax.fori_loop(0, n_sub, per_sub, 0)


def _post_kernel(h_ref, gm_ref, p_ref, q_ref, kp_ref, ko_ref, vp_ref, vo_ref, k_any, v_any,
                 w_mo_ref, n2_ref, w_in_ref, w_out_ref, np_ref, w_gate_ref, w_proj_ref, nf_ref,
                 o_ref, act_ref, sb_ref, acc_ref, decay_ref, need_ref, kbuf, vbuf, sems,
                 *, final, tiles_per_seq):
    j = pl.program_id(0)
    n_tiles = pl.num_programs(0) - 1
    cur = j % 2

    @pl.when(j == 0)
    def _():
        sb_ref[1] = jnp.zeros(sb_ref.shape[1:], sb_ref.dtype)

    tile = jnp.minimum(j, n_tiles - 1)
    b, qi = tile // tiles_per_seq, tile % tiles_per_seq
    attention = _attention_windows(q_ref, kp_ref, ko_ref, vp_ref, vo_ref, sb_ref.at[cur],
                                   acc_ref, decay_ref, need_ref, qi, ATTN_GROUPS)

    mixed = jnp.concatenate([gm_ref[...], sb_ref[1 - cur]], axis=1)
    h = h_ref[...] + _dot(mixed, w_mo_ref[...])
    y = _rms(h, n2_ref[...]).astype(BF16)
    h = h + 0.5 * _half_swiglu(y, w_in_ref, w_out_ref, act_ref, attention[:-2])
    attention[-2]()
    gate = jax.nn.sigmoid(_dot(_rms(h, np_ref[...]).astype(BF16), w_gate_ref[...]))
    attention[-1]()
    h = h + gate * _dot(p_ref[...].astype(BF16), w_proj_ref[...])
    o_ref[...] = _rms(h, nf_ref[...]) if final else h

    @pl.when(need_ref[need_ref.shape[0] - 1] != 0)
    def _():
        _attention_sweeps(q_ref, kp_ref, ko_ref, vp_ref, vo_ref, k_any, v_any, sb_ref.at[cur],
                          acc_ref, decay_ref, need_ref, kbuf, vbuf, sems, b, qi)


def _post_call(h1, gm, p2, q, k, v, w_mo, n2, w_in, w_out, np_, w_gate, w_proj, nf, final):
    tokens = h1.shape[0]
    batch, seq, _ = q.shape
    tm = TOKEN_TILE
    n_tiles = tokens // tm
    tiles_per_seq = seq // tm
    sub_per_tile = tm // ATTN_SUB
    dense = lambda width: pl.BlockSpec((tm, width), lambda j: (jnp.maximum(j - 1, 0), 0))

    def own(j):
        t = jnp.minimum(j, n_tiles - 1)
        return t // tiles_per_seq, t % tiles_per_seq, 0

    def prev(j):
        b, qi, _ = own(j)
        return b, jnp.maximum(qi * (sub_per_tile // 2) - 1, 0), 0

    own_spec = pl.BlockSpec((1, tm, SB_WIDTH), own)
    prev_spec = pl.BlockSpec((1, 2 * ATTN_SUB, SB_WIDTH), prev)
    hbm = pl.BlockSpec(memory_space=pl.ANY)
    return pl.pallas_call(
        functools.partial(_post_kernel, final=final, tiles_per_seq=tiles_per_seq),
        grid=(n_tiles + 1,),
        in_specs=[
            dense(D_MODEL), dense(GM_WIDTH), dense(PLE_DIM),
            own_spec, prev_spec, own_spec, prev_spec, own_spec, hbm, hbm,
            _resident(w_mo.shape),
            _resident((1, D_MODEL)),
            _resident(w_in.shape),
            _resident(w_out.shape),
            _resident((1, D_MODEL)),
            _resident(w_gate.shape),
            _resident(w_proj.shape),
            _resident((1, D_MODEL)),
        ],
        out_specs=dense(D_MODEL),
        out_shape=jax.ShapeDtypeStruct((tokens, D_MODEL), F32),
        scratch_shapes=[
            pltpu.VMEM((tm, D_FF), BF16),
            pltpu.VMEM((2, tm, SB_WIDTH), BF16),
            pltpu.VMEM((tm, SB_WIDTH), F32),
            pltpu.VMEM((sub_per_tile, 2 * ATTN_SUB, LANES), F32),
            pltpu.SMEM((SB_WIDTH // LANES * sub_per_tile + 1,), jnp.int32),
            pltpu.VMEM((ATTN_SUB, LANES), BF16),
            pltpu.VMEM((ATTN_SUB, LANES), BF16),
            pltpu.SemaphoreType.DMA((2,)),
        ],
        compiler_params=pltpu.CompilerParams(
            dimension_semantics=("arbitrary",), vmem_limit_bytes=VMEM_LIMIT),
        name="post_attn_mix_out_ffn_ple",
    )(h1, gm, p2, q, k, k, v, v, k, v, w_mo, n2, w_in, w_out, np_, w_gate, w_proj, nf)


def kernel(x, p, ffn1_norm, ffn1_w_in, ffn1_w_out, mix_norm, w_mix_in, gmlp_v_norm, gmlp_w_s, gmlp_b, w_mix_out, ffn2_norm, ffn2_w_in, ffn2_w_out, ple_norm, ple_w_gate, ple_w_proj, final_norm):
    batch, seq, _ = x.shape
    depth = p.shape[0]
    tokens = batch * seq
    vec = lambda g: g.reshape(1, -1).astype(F32)

    h = x.reshape(tokens, D_MODEL)
    for i in range(depth):
        bs_full = jnp.repeat(gmlp_b[i].T, GM_HEAD_DIM, axis=1)
        h1, gm, q, k, v, w_mo, w_in2, w_out2, w_gate, w_proj = _pre_call(
            h, vec(ffn1_norm[i]), ffn1_w_in, ffn1_w_out, vec(mix_norm[i]), w_mix_in,
            vec(gmlp_v_norm[i]), gmlp_w_s[i], bs_full,
            [w_mix_out, ffn2_w_in, ffn2_w_out, ple_w_gate, ple_w_proj], layer=i)
        shape3 = (batch, seq, SB_WIDTH)
        h = _post_call(
            h1, gm, p[i].reshape(tokens, PLE_DIM),
            q.reshape(shape3), k.reshape(shape3), v.reshape(shape3),
            w_mo, vec(ffn2_norm[i]), w_in2, w_out2, vec(ple_norm[i]), w_gate, w_proj,
            vec(final_norm), final=(i == depth - 1))
    return h.reshape(batch, seq, D_MODEL)
```

```python
import functools

import jax
import jax.numpy as jnp
from jax import lax
from jax.experimental import pallas as pl
from jax.experimental.pallas import tpu as pltpu

D_MODEL = 1024
PLE_DIM = 256
D_FF = 2816
CHUNK = 128
GM_HEADS = 4
GM_HEAD_DIM = 128
GM_WIDTH = GM_HEADS * GM_HEAD_DIM
SB_HEADS = 8
SB_HEAD_DIM = 64
SB_WIDTH = SB_HEADS * SB_HEAD_DIM
EPS = 1e-6

LANES = 128
FF_CHUNK = 256
TOKEN_TILE = 512
CAST_ROWS = 16
STAGE_CHUNKS = 16
ATTN_SUB = 128
ATTN_GROUPS = 8
LOG2E = 1.4426950408889634
DECAY_LIMIT = 127.0
VMEM_LIMIT = 56 * 1024 * 1024

BF16 = jnp.bfloat16
F32 = jnp.float32


def _dot(a, b):
    return jnp.dot(a, b, preferred_element_type=F32)


def _rms(x, g):
    return x * lax.rsqrt(jnp.mean(x * x, axis=-1, keepdims=True) + EPS) * g


def _half_swiglu(y, w_in_ref, w_out_ref, act_ref, side_work=()):
    for c in range(D_FF // FF_CHUNK):
        if c < len(side_work):
            side_work[c]()
        lo = c * FF_CHUNK
        gate = _dot(y, w_in_ref[:, lo:lo + FF_CHUNK])
        up = _dot(y, w_in_ref[:, D_FF + lo:D_FF + lo + FF_CHUNK])
        act_ref[:, lo:lo + FF_CHUNK] = (gate * jax.nn.sigmoid(gate) * up).astype(BF16)
    return _dot(act_ref[...], w_out_ref[...])


def _gelu(x):
    return 0.5 * x * (1.0 + lax.erf(x * (2.0 ** -0.5)))


def _stage_weights(weights, layer):
    def chunk(w, c):
        src, _, stage, sems = w
        rows = stage.shape[1]
        slot = c % 2
        return pltpu.make_async_copy(
            src.at[layer, pl.ds(pl.multiple_of(c * rows, rows), rows)], stage.at[slot],
            sems.at[slot])

    for w in weights:
        chunk(w, 0).start()

    def body(c, carry):
        @pl.when(c + 1 < STAGE_CHUNKS)
        def _():
            for w in weights:
                chunk(w, c + 1).start()

        for w in weights:
            _, dst, stage, _ = w
            rows = stage.shape[1]
            chunk(w, c).wait()
            dst[pl.ds(pl.multiple_of(c * rows, rows), rows), :] = stage[c % 2].astype(BF16)
        return carry

    lax.fori_loop(0, STAGE_CHUNKS, body, 0)


def _pre_kernel(x_ref, n1_ref, w_in_hbm, w_out_hbm, nm_ref, w_mix_hbm, vg_ref, ws_ref, bs_ref,
                *refs, layer):
    n_cast = (len(refs) - 13) // 2
    cast_in, refs = refs[:n_cast], refs[n_cast:]
    h_ref, gm_ref, kvq_ref = refs[:3]
    cast_out = refs[3:3 + n_cast]
    act_ref, w_in_ref, w_out_ref, w_mix_ref = refs[3 + n_cast:7 + n_cast]
    stages, sems = refs[7 + n_cast:10 + n_cast], refs[10 + n_cast:]

    @pl.when(pl.program_id(0) == 0)
    def _():
        _stage_weights(list(zip((w_in_hbm, w_out_hbm, w_mix_hbm),
                                (w_in_ref, w_out_ref, w_mix_ref), stages, sems)), layer)

    for src, dst in zip(cast_in, cast_out):
        dst[...] = src[...].astype(BF16)

    x = x_ref[...]
    y = _rms(x, n1_ref[...]).astype(BF16)
    h = x + 0.5 * _half_swiglu(y, w_in_ref, w_out_ref, act_ref)
    h_ref[...] = h

    n = _rms(h, nm_ref[...]).astype(BF16)
    zu = _dot(n, w_mix_ref[:, 0:GM_WIDTH])
    zv = _dot(n, w_mix_ref[:, GM_WIDTH:2 * GM_WIDTH])

    base = 2 * GM_WIDTH
    att_q = _dot(n, w_mix_ref[:, base:base + SB_WIDTH]) * (SB_HEAD_DIM ** -0.5 * LOG2E)
    att_k = _dot(n, w_mix_ref[:, base + SB_WIDTH:base + 2 * SB_WIDTH])
    att_v = _dot(n, w_mix_ref[:, base + 2 * SB_WIDTH:base + 3 * SB_WIDTH])
    for i, part in enumerate((att_k, att_v, att_q)):
        kvq_ref[:, i * SB_WIDTH:(i + 1) * SB_WIDTH] = part.astype(BF16)

    u = _gelu(zu)
    v = _gelu(zv)
    vn = _rms(v, vg_ref[...]).astype(BF16)

    t_idx = lax.broadcasted_iota(jnp.int32, (CHUNK, CHUNK), 0)
    s_idx = lax.broadcasted_iota(jnp.int32, (CHUNK, CHUNK), 1)
    keep = s_idx <= t_idx
    for hd in range(GM_HEADS):
        w = jnp.where(keep, ws_ref[hd], 0.0).astype(BF16)
        cols = slice(hd * GM_HEAD_DIM, (hd + 1) * GM_HEAD_DIM)
        chunks = [slice(c * CHUNK, (c + 1) * CHUNK) for c in range(x.shape[0] // CHUNK)]
        sv = _dot(w, jnp.concatenate([vn[rows, cols] for rows in chunks], axis=1))
        for rows in chunks:
            gm_ref[rows, cols] = (u[rows, cols] * (sv[:, rows] + bs_ref[:, cols])).astype(BF16)


def _resident(shape):
    return pl.BlockSpec(shape, lambda *_: (0,) * len(shape), pipeline_mode=pl.Buffered(1))


def _cast_blocks(rows, steps):
    tiles = rows // CAST_ROWS
    return max(d for d in range(1, min(steps, tiles) + 1) if tiles % d == 0)


def _stage_rows(rows):
    assert rows % (STAGE_CHUNKS * CAST_ROWS) == 0
    return rows // STAGE_CHUNKS


def _pre_call(x2, n1, w_in, w_out, nm, w_mix, vg, ws, bs_full, next_weights, layer):
    tokens = x2.shape[0]
    tm = TOKEN_TILE
    steps = tokens // tm
    row = lambda width: pl.BlockSpec((tm, width), lambda i: (i, 0))
    hbm = pl.BlockSpec(memory_space=pl.ANY)
    cast_in, cast_out = [], []
    for w in next_weights:
        _, rows, cols = w.shape
        blocks = _cast_blocks(rows, steps)
        cast_in.append(pl.BlockSpec(
            (None, rows // blocks, cols),
            lambda i, last=blocks - 1: (layer, jnp.minimum(i, last), 0)))
        cast_out.append(pl.BlockSpec(
            (rows // blocks, cols), lambda i, last=blocks - 1: (jnp.minimum(i, last), 0)))
    own = (w_in, w_out, w_mix)
    return pl.pallas_call(
        functools.partial(_pre_kernel, layer=layer),
        grid=(steps,),
        in_specs=[
            row(D_MODEL),
            _resident((1, D_MODEL)),
            hbm,
            hbm,
            _resident((1, D_MODEL)),
            hbm,
            _resident((1, GM_WIDTH)),
            _resident(ws.shape),
            _resident(bs_full.shape),
        ] + cast_in,
        out_specs=[row(D_MODEL), row(GM_WIDTH), row(3 * SB_WIDTH)] + cast_out,
        out_shape=[
            jax.ShapeDtypeStruct((tokens, D_MODEL), F32),
            jax.ShapeDtypeStruct((tokens, GM_WIDTH), BF16),
            jax.ShapeDtypeStruct((tokens, 3 * SB_WIDTH), BF16),
        ] + [jax.ShapeDtypeStruct(w.shape[1:], BF16) for w in next_weights],
        scratch_shapes=[pltpu.VMEM((tm, D_FF), BF16)]
        + [pltpu.VMEM(w.shape[1:], BF16) for w in own]
        + [pltpu.VMEM((2, _stage_rows(w.shape[1]), w.shape[2]), F32) for w in own]
        + [pltpu.SemaphoreType.DMA((2,)) for _ in own],
        compiler_params=pltpu.CompilerParams(
            dimension_semantics=("arbitrary",), vmem_limit_bytes=VMEM_LIMIT),
        name="pre_ffn_mix_in",
    )(x2, n1, w_in, w_out, nm, w_mix, vg, ws, bs_full, *next_weights)


def _later_matrix(n):
    j = lax.broadcasted_iota(jnp.int32, (n, n), 0)
    s = lax.broadcasted_iota(jnp.int32, (n, n), 1)
    return (j > s).astype(BF16)


def _stack_heads(x):
    first = lax.broadcasted_iota(jnp.int32, x.shape, 1) < SB_HEAD_DIM
    zero = jnp.zeros_like(x)
    return jnp.concatenate([jnp.where(first, x, zero), jnp.where(first, zero, x)], axis=0)


def _scores(qs, kb):
    return lax.dot_general(qs, kb, (((1,), (1,)), ((), ())), preferred_element_type=F32)


def _softplus2(z):
    return jnp.maximum(z, 0.0) + jnp.log2(1.0 + jnp.exp2(-jnp.abs(z)))


def _mask_window(x, diff, left_valid):
    left, right = x[:, :ATTN_SUB], x[:, ATTN_SUB:]
    if left_valid is not True:
        left = jnp.where(left_valid, left, 0.0)
    right = jnp.where(diff < 0, right, 0.0)
    return jnp.concatenate([left, right], axis=1)


def _later_sum(sp, later):
    return _dot(sp.astype(BF16), later)


def _weights(z, sp, decay):
    return jnp.exp2((z - sp) - decay)


def _apply(a, vb):
    pv = _dot(a.astype(BF16), vb)
    first = lax.broadcasted_iota(jnp.int32, (ATTN_SUB, LANES), 1) < SB_HEAD_DIM
    return jnp.where(first, pv[:ATTN_SUB], pv[ATTN_SUB:])


def _window_consts():
    sub = ATTN_SUB
    diff = (lax.broadcasted_iota(jnp.int32, (2 * sub, sub), 1)
            - lax.broadcasted_iota(jnp.int32, (2 * sub, sub), 0) % sub)
    return _later_matrix(2 * sub), diff


def _attention_windows(q_ref, kp_ref, ko_ref, vp_ref, vo_ref, out_ref, acc_ref, decay_ref,
                       need_ref, qi, n_groups):
    sub = ATTN_SUB
    n_sub = q_ref.shape[1] // sub
    all_units = [(g, s) for g in range(SB_WIDTH // LANES) for s in range(n_sub)]
    per_group = len(all_units) // n_groups
    lanes = lambda g: slice(g * LANES, (g + 1) * LANES)

    def window(prev_ref, own_ref, g, s):
        if s == 0:
            return jnp.concatenate([prev_ref[0, sub:2 * sub, lanes(g)],
                                    own_ref[0, 0:sub, lanes(g)]], axis=0)
        return own_ref[0, (s - 1) * sub:(s + 1) * sub, lanes(g)]

    later_win, diff = _window_consts()
    any_need = len(all_units)
    need_ref[any_need] = 0
    groups = [all_units[k * per_group:(k + 1) * per_group] for k in range(n_groups)]
    left_valid = [[qi > 0 if s == 0 else True for _, s in units] for units in groups]
    state = [{} for _ in groups]

    def stage_scores(k):
        units, st = groups[k], state[k]
        st["z"] = [_scores(_stack_heads(q_ref[0, s * sub:(s + 1) * sub, lanes(g)]),
                           window(kp_ref, ko_ref, g, s)) for g, s in units]
        st["sp"] = [_softplus2(x) for x in st["z"]]
        spm = [_mask_window(x, diff, lv) for x, lv in zip(st["sp"], left_valid[k])]
        st["spm"] = [x.astype(BF16) for x in spm]
        for u, (g, s) in enumerate(units):
            decay = jnp.sum(spm[u], axis=1, keepdims=True)
            decay_ref[s, :, g:g + 1] = decay
            unfinished = jnp.logical_and(jnp.min(decay) < DECAY_LIMIT, qi * n_sub + s >= 2)
            need_ref[k * per_group + u] = unfinished.astype(jnp.int32)
            need_ref[any_need] = need_ref[any_need] | unfinished.astype(jnp.int32)

    def stage_later(k):
        st = state[k]
        stacked = _dot(jnp.concatenate(st["spm"], axis=0), later_win)
        later = [stacked[u * 2 * sub:(u + 1) * 2 * sub] for u in range(per_group)]
        st["a"] = [_mask_window(_weights(*zsl), diff, lv).astype(BF16)
                   for zsl, lv in zip(zip(st["z"], st["sp"], later), left_valid[k])]

    def stage_values(k):
        for a, (g, s) in zip(state[k]["a"], groups[k]):
            pv = _apply(a, window(vp_ref, vo_ref, g, s))
            acc_ref[s * sub:(s + 1) * sub, lanes(g)] = pv
            out_ref[s * sub:(s + 1) * sub, lanes(g)] = pv.astype(out_ref.dtype)

    def thunk(t):
        for stage, k in ((stage_values, t - 2), (stage_later, t - 1), (stage_scores, t)):
            if 0 <= k < n_groups:
                stage(k)

    return [functools.partial(thunk, t) for t in range(n_groups + 2)]


def _attention_sweeps(q_ref, kp_ref, ko_ref, vp_ref, vo_ref, k_any, v_any, out_ref, acc_ref,
                      decay_ref, need_ref, kbuf, vbuf, sems, b, qi):
    sub = ATTN_SUB
    n_sub = q_ref.shape[1] // sub
    later_sub = _later_matrix(sub)

    for g in range(SB_WIDTH // LANES):
        lanes = slice(g * LANES, (g + 1) * LANES)

        def fetch(first_key):
            copies = [pltpu.make_async_copy(src.at[b, pl.ds(first_key, sub), lanes], dst, sems.at[i])
                      for i, (src, dst) in enumerate(((k_any, kbuf), (v_any, vbuf)))]
            for c in copies:
                c.start()
            for c in copies:
                c.wait()

        def per_sub(s, carry):
            @pl.when(need_ref[g * n_sub + s] != 0)
            def _():
                tile = qi * n_sub + s
                rows = pl.ds(pl.multiple_of(s * sub, sub), sub)
                qs = _stack_heads(q_ref[0, rows, lanes])

                def block(kb, vb, acc, decay):
                    z = _scores(qs, kb)
                    sp = _softplus2(z)
                    a = _weights(z, sp, _later_sum(sp, later_sub) + decay)
                    return acc + _apply(a, vb), decay + jnp.sum(sp, axis=1, keepdims=True)

                def resident(prev_ref, own_ref):
                    in_own = pl.ds(pl.multiple_of(jnp.maximum(s - 2, 0) * sub, sub), sub)
                    in_prev = pl.ds(pl.multiple_of(jnp.minimum(s, 1) * sub, sub), sub)
                    return lax.cond(s >= 2, lambda: own_ref[0, in_own, lanes],
                                    lambda: prev_ref[0, in_prev, lanes])

                acc, decay = block(resident(kp_ref, ko_ref), resident(vp_ref, vo_ref),
                                   acc_ref[rows, lanes], decay_ref[s, :, g:g + 1])

                def cond(st):
                    return jnp.logical_and(st[0] >= 0, st[1])

                def body(st):
                    j, _, acc, decay = st
                    fetch(pl.multiple_of(j * sub, sub))
                    acc, decay = block(kbuf[...], vbuf[...], acc, decay)
                    return j - 1, jnp.min(decay) < DECAY_LIMIT, acc, decay

                st = lax.while_loop(cond, body,
                                    (tile - 3, jnp.min(decay) < DECAY_LIMIT, acc, decay))
                out_ref[rows, lanes] = st[2].astype(out_ref.dtype)

            return carry

        lax.fori_loop(0, n_sub, per_sub, 0)


def _post_kernel(h_ref, gm_ref, p_ref, own_ref, prev_ref, kvq_any,
                 w_mo_ref, n2_ref, w_in_ref, w_out_ref, np_ref, w_gate_ref, w_proj_ref, nf_ref,
                 o_ref, act_ref, sb_ref, acc_ref, decay_ref, need_ref, kbuf, vbuf, sems,
                 *, final, tiles_per_seq):
    j = pl.program_id(0)
    n_tiles = pl.num_programs(0) - 1
    cur = j % 2
    part = lambda ref, i: ref.at[:, :, i * SB_WIDTH:(i + 1) * SB_WIDTH]
    ko_ref, vo_ref, q_ref = (part(own_ref, i) for i in range(3))
    kp_ref, vp_ref = part(prev_ref, 0), part(prev_ref, 1)
    k_any, v_any = part(kvq_any, 0), part(kvq_any, 1)

    @pl.when(j == 0)
    def _():
        sb_ref[1] = jnp.zeros(sb_ref.shape[1:], sb_ref.dtype)

    tile = jnp.minimum(j, n_tiles - 1)
    b, qi = tile // tiles_per_seq, tile % tiles_per_seq
    attention = _attention_windows(q_ref, kp_ref, ko_ref, vp_ref, vo_ref, sb_ref.at[cur],
                                   acc_ref, decay_ref, need_ref, qi, ATTN_GROUPS)

    mixed = jnp.concatenate([gm_ref[...], sb_ref[1 - cur]], axis=1)
    h = h_ref[...] + _dot(mixed, w_mo_ref[...])
    y = _rms(h, n2_ref[...]).astype(BF16)
    h = h + 0.5 * _half_swiglu(y, w_in_ref, w_out_ref, act_ref, attention[:-2])
    attention[-2]()
    gate = jax.nn.sigmoid(_dot(_rms(h, np_ref[...]).astype(BF16), w_gate_ref[...]))
    attention[-1]()
    h = h + gate * _dot(p_ref[...].astype(BF16), w_proj_ref[...])
    o_ref[...] = _rms(h, nf_ref[...]) if final else h

    @pl.when(need_ref[need_ref.shape[0] - 1] != 0)
    def _():
        _attention_sweeps(q_ref, kp_ref, ko_ref, vp_ref, vo_ref, k_any, v_any, sb_ref.at[cur],
                          acc_ref, decay_ref, need_ref, kbuf, vbuf, sems, b, qi)


def _post_call(h1, gm, p2, kvq, w_mo, n2, w_in, w_out, np_, w_gate, w_proj, nf, final):
    tokens = h1.shape[0]
    batch, seq, _ = kvq.shape
    tm = TOKEN_TILE
    n_tiles = tokens // tm
    tiles_per_seq = seq // tm
    sub_per_tile = tm // ATTN_SUB
    dense = lambda width: pl.BlockSpec((tm, width), lambda j: (jnp.maximum(j - 1, 0), 0))

    def own(j):
        t = jnp.minimum(j, n_tiles - 1)
        return t // tiles_per_seq, t % tiles_per_seq, 0

    def prev(j):
        b, qi, _ = own(j)
        return b, jnp.maximum(qi * (sub_per_tile // 2) - 1, 0), 0

    own_spec = pl.BlockSpec((1, tm, 3 * SB_WIDTH), own)
    prev_spec = pl.BlockSpec((1, 2 * ATTN_SUB, 2 * SB_WIDTH), prev)
    hbm = pl.BlockSpec(memory_space=pl.ANY)
    return pl.pallas_call(
        functools.partial(_post_kernel, final=final, tiles_per_seq=tiles_per_seq),
        grid=(n_tiles + 1,),
        in_specs=[
            dense(D_MODEL), dense(GM_WIDTH), dense(PLE_DIM),
            own_spec, prev_spec, hbm,
            _resident(w_mo.shape),
            _resident((1, D_MODEL)),
            _resident(w_in.shape),
            _resident(w_out.shape),
            _resident((1, D_MODEL)),
            _resident(w_gate.shape),
            _resident(w_proj.shape),
            _resident((1, D_MODEL)),
        ],
        out_specs=dense(D_MODEL),
        out_shape=jax.ShapeDtypeStruct((tokens, D_MODEL), F32),
        scratch_shapes=[
            pltpu.VMEM((tm, D_FF), BF16),
            pltpu.VMEM((2, tm, SB_WIDTH), BF16),
            pltpu.VMEM((tm, SB_WIDTH), F32),
            pltpu.VMEM((sub_per_tile, 2 * ATTN_SUB, LANES), F32),
            pltpu.SMEM((SB_WIDTH // LANES * sub_per_tile + 1,), jnp.int32),
            pltpu.VMEM((ATTN_SUB, LANES), BF16),
            pltpu.VMEM((ATTN_SUB, LANES), BF16),
            pltpu.SemaphoreType.DMA((2,)),
        ],
        compiler_params=pltpu.CompilerParams(
            dimension_semantics=("arbitrary",), vmem_limit_bytes=VMEM_LIMIT),
        name="post_attn_mix_out_ffn_ple",
    )(h1, gm, p2, kvq, kvq, kvq, w_mo, n2, w_in, w_out, np_, w_gate, w_proj, nf)


def kernel(x, p, ffn1_norm, ffn1_w_in, ffn1_w_out, mix_norm, w_mix_in, gmlp_v_norm, gmlp_w_s, gmlp_b, w_mix_out, ffn2_norm, ffn2_w_in, ffn2_w_out, ple_norm, ple_w_gate, ple_w_proj, final_norm):
    batch, seq, _ = x.shape
    depth = p.shape[0]
    tokens = batch * seq
    vec = lambda g: g.reshape(1, -1).astype(F32)

    h = x.reshape(tokens, D_MODEL)
    for i in range(depth):
        bs_full = jnp.repeat(gmlp_b[i].T, GM_HEAD_DIM, axis=1)
        h1, gm, kvq, w_mo, w_in2, w_out2, w_gate, w_proj = _pre_call(
            h, vec(ffn1_norm[i]), ffn1_w_in, ffn1_w_out, vec(mix_norm[i]), w_mix_in,
            vec(gmlp_v_norm[i]), gmlp_w_s[i], bs_full,
            [w_mix_out, ffn2_w_in, ffn2_w_out, ple_w_gate, ple_w_proj], layer=i)
        h = _post_call(
            h1, gm, p[i].reshape(tokens, PLE_DIM), kvq.reshape(batch, seq, 3 * SB_WIDTH),
            w_mo, vec(ffn2_norm[i]), w_in2, w_out2, vec(ple_norm[i]), w_gate, w_proj,
            vec(final_norm), final=(i == depth - 1))
    return h.reshape(batch, seq, D_MODEL)
```

```python
import functools

import jax
import jax.numpy as jnp
from jax import lax
from jax.experimental import pallas as pl
from jax.experimental.pallas import tpu as pltpu

D_MODEL = 1024
PLE_DIM = 256
D_FF = 2816
CHUNK = 128
GM_HEADS = 4
GM_HEAD_DIM = 128
GM_WIDTH = GM_HEADS * GM_HEAD_DIM
SB_HEADS = 8
SB_HEAD_DIM = 64
SB_WIDTH = SB_HEADS * SB_HEAD_DIM
EPS = 1e-6

LANES = 128
FF_CHUNK = 256
TOKEN_TILE = 512
CAST_ROWS = 16
STAGE_CHUNKS = 16
ATTN_SUB = 128
ATTN_GROUPS = 8
LOG2E = 1.4426950408889634
DECAY_LIMIT = 127.0
VMEM_LIMIT = 56 * 1024 * 1024

BF16 = jnp.bfloat16
F32 = jnp.float32


def _dot(a, b):
    return jnp.dot(a, b, preferred_element_type=F32)


def _inv_rms(x):
    return lax.rsqrt(jnp.mean(x * x, axis=-1, keepdims=True) + EPS)


def _rms(x, g):
    return x * _inv_rms(x) * g


def _half_swiglu(x, g, w_in_ref, w_out_ref, act_ref, side_work=()):
    y = (x * g).astype(BF16)
    row_scale = _inv_rms(x)
    for c in range(D_FF // FF_CHUNK):
        if c < len(side_work):
            side_work[c]()
        lo = c * FF_CHUNK
        gate = _dot(y, w_in_ref[:, lo:lo + FF_CHUNK])
        up = _dot(y, w_in_ref[:, D_FF + lo:D_FF + lo + FF_CHUNK])
        gate, up = gate * row_scale, up * row_scale
        act_ref[:, lo:lo + FF_CHUNK] = (gate * jax.nn.sigmoid(gate) * up).astype(BF16)
    return _dot(act_ref[...], w_out_ref[...])


def _gelu(x):
    return 0.5 * x * (1.0 + lax.erf(x * (2.0 ** -0.5)))


def _stage_weights(weights, layer):
    def chunk(w, c):
        src, _, stage, sems = w
        rows = stage.shape[1]
        slot = c % 2
        return pltpu.make_async_copy(
            src.at[layer, pl.ds(pl.multiple_of(c * rows, rows), rows)], stage.at[slot],
            sems.at[slot])

    for w in weights:
        chunk(w, 0).start()

    def body(c, carry):
        @pl.when(c + 1 < STAGE_CHUNKS)
        def _():
            for w in weights:
                chunk(w, c + 1).start()

        for w in weights:
            _, dst, stage, _ = w
            rows = stage.shape[1]
            chunk(w, c).wait()
            dst[pl.ds(pl.multiple_of(c * rows, rows), rows), :] = stage[c % 2].astype(BF16)
        return carry

    lax.fori_loop(0, STAGE_CHUNKS, body, 0)


def _pre_kernel(x_ref, n1_ref, w_in_hbm, w_out_hbm, nm_ref, w_mix_hbm, vg_ref, ws_ref, bs_ref,
                *refs, layer):
    n_cast = (len(refs) - 13) // 2
    cast_in, refs = refs[:n_cast], refs[n_cast:]
    h_ref, gm_ref, kvq_ref = refs[:3]
    cast_out = refs[3:3 + n_cast]
    act_ref, w_in_ref, w_out_ref, w_mix_ref = refs[3 + n_cast:7 + n_cast]
    stages, sems = refs[7 + n_cast:10 + n_cast], refs[10 + n_cast:]

    @pl.when(pl.program_id(0) == 0)
    def _():
        _stage_weights(list(zip((w_in_hbm, w_out_hbm, w_mix_hbm),
                                (w_in_ref, w_out_ref, w_mix_ref), stages, sems)), layer)

    for src, dst in zip(cast_in, cast_out):
        dst[...] = src[...].astype(BF16)

    x = x_ref[...]
    h = x + 0.5 * _half_swiglu(x, n1_ref[...], w_in_ref, w_out_ref, act_ref)
    h_ref[...] = h

    n = _rms(h, nm_ref[...]).astype(BF16)
    zu = _dot(n, w_mix_ref[:, 0:GM_WIDTH])
    zv = _dot(n, w_mix_ref[:, GM_WIDTH:2 * GM_WIDTH])

    base = 2 * GM_WIDTH
    att_q = _dot(n, w_mix_ref[:, base:base + SB_WIDTH]) * (SB_HEAD_DIM ** -0.5 * LOG2E)
    att_k = _dot(n, w_mix_ref[:, base + SB_WIDTH:base + 2 * SB_WIDTH])
    att_v = _dot(n, w_mix_ref[:, base + 2 * SB_WIDTH:base + 3 * SB_WIDTH])
    for i, part in enumerate((att_k, att_v, att_q)):
        kvq_ref[:, i * SB_WIDTH:(i + 1) * SB_WIDTH] = part.astype(BF16)

    u = _gelu(zu)
    v = _gelu(zv)
    vn = _rms(v, vg_ref[...]).astype(BF16)

    t_idx = lax.broadcasted_iota(jnp.int32, (CHUNK, CHUNK), 0)
    s_idx = lax.broadcasted_iota(jnp.int32, (CHUNK, CHUNK), 1)
    keep = s_idx <= t_idx
    for hd in range(GM_HEADS):
        w = jnp.where(keep, ws_ref[hd], 0.0).astype(BF16)
        cols = slice(hd * GM_HEAD_DIM, (hd + 1) * GM_HEAD_DIM)
        chunks = [slice(c * CHUNK, (c + 1) * CHUNK) for c in range(x.shape[0] // CHUNK)]
        sv = _dot(w, jnp.concatenate([vn[rows, cols] for rows in chunks], axis=1))
        for rows in chunks:
            gm_ref[rows, cols] = (u[rows, cols] * (sv[:, rows] + bs_ref[:, cols])).astype(BF16)


def _resident(shape):
    return pl.BlockSpec(shape, lambda *_: (0,) * len(shape), pipeline_mode=pl.Buffered(1))


def _cast_blocks(rows, steps):
    tiles = rows // CAST_ROWS
    return max(d for d in range(1, min(steps, tiles) + 1) if tiles % d == 0)


def _stage_rows(rows):
    assert rows % (STAGE_CHUNKS * CAST_ROWS) == 0
    return rows // STAGE_CHUNKS


def _pre_call(x2, n1, w_in, w_out, nm, w_mix, vg, ws, bs_full, next_weights, layer):
    tokens = x2.shape[0]
    tm = TOKEN_TILE
    steps = tokens // tm
    row = lambda width: pl.BlockSpec((tm, width), lambda i: (i, 0))
    hbm = pl.BlockSpec(memory_space=pl.ANY)
    cast_in, cast_out = [], []
    for w in next_weights:
        _, rows, cols = w.shape
        blocks = _cast_blocks(rows, steps)
        cast_in.append(pl.BlockSpec(
            (None, rows // blocks, cols),
            lambda i, last=blocks - 1: (layer, jnp.minimum(i, last), 0)))
        cast_out.append(pl.BlockSpec(
            (rows // blocks, cols), lambda i, last=blocks - 1: (jnp.minimum(i, last), 0)))
    own = (w_in, w_out, w_mix)
    return pl.pallas_call(
        functools.partial(_pre_kernel, layer=layer),
        grid=(steps,),
        in_specs=[
            row(D_MODEL),
            _resident((1, D_MODEL)),
            hbm,
            hbm,
            _resident((1, D_MODEL)),
            hbm,
            _resident((1, GM_WIDTH)),
            _resident(ws.shape),
            _resident(bs_full.shape),
        ] + cast_in,
        out_specs=[row(D_MODEL), row(GM_WIDTH), row(3 * SB_WIDTH)] + cast_out,
        out_shape=[
            jax.ShapeDtypeStruct((tokens, D_MODEL), F32),
            jax.ShapeDtypeStruct((tokens, GM_WIDTH), BF16),
            jax.ShapeDtypeStruct((tokens, 3 * SB_WIDTH), BF16),
        ] + [jax.ShapeDtypeStruct(w.shape[1:], BF16) for w in next_weights],
        scratch_shapes=[pltpu.VMEM((tm, D_FF), BF16)]
        + [pltpu.VMEM(w.shape[1:], BF16) for w in own]
        + [pltpu.VMEM((2, _stage_rows(w.shape[1]), w.shape[2]), F32) for w in own]
        + [pltpu.SemaphoreType.DMA((2,)) for _ in own],
        compiler_params=pltpu.CompilerParams(
            dimension_semantics=("arbitrary",), vmem_limit_bytes=VMEM_LIMIT),
        name="pre_ffn_mix_in",
    )(x2, n1, w_in, w_out, nm, w_mix, vg, ws, bs_full, *next_weights)


def _later_matrix(n):
    j = lax.broadcasted_iota(jnp.int32, (n, n), 0)
    s = lax.broadcasted_iota(jnp.int32, (n, n), 1)
    return (j > s).astype(BF16)


def _stack_heads(x):
    first = lax.broadcasted_iota(jnp.int32, x.shape, 1) < SB_HEAD_DIM
    zero = jnp.zeros_like(x)
    return jnp.concatenate([jnp.where(first, x, zero), jnp.where(first, zero, x)], axis=0)


def _scores(qs, kb):
    return lax.dot_general(qs, kb, (((1,), (1,)), ((), ())), preferred_element_type=F32)


def _softplus2(z):
    return jnp.maximum(z, 0.0) + jnp.log2(1.0 + jnp.exp2(-jnp.abs(z)))


def _mask_window(x, diff, left_valid):
    left, right = x[:, :ATTN_SUB], x[:, ATTN_SUB:]
    if left_valid is not True:
        left = jnp.where(left_valid, left, 0.0)
    right = jnp.where(diff < 0, right, 0.0)
    return jnp.concatenate([left, right], axis=1)


def _later_sum(sp, later):
    return _dot(sp.astype(BF16), later)


def _weights(z, sp, decay):
    return jnp.exp2((z - sp) - decay)


def _apply(a, vb):
    pv = _dot(a.astype(BF16), vb)
    first = lax.broadcasted_iota(jnp.int32, (ATTN_SUB, LANES), 1) < SB_HEAD_DIM
    return jnp.where(first, pv[:ATTN_SUB], pv[ATTN_SUB:])


def _window_consts():
    sub = ATTN_SUB
    diff = (lax.broadcasted_iota(jnp.int32, (2 * sub, sub), 1)
            - lax.broadcasted_iota(jnp.int32, (2 * sub, sub), 0) % sub)
    return _later_matrix(2 * sub), diff


def _attention_windows(q_ref, kp_ref, ko_ref, vp_ref, vo_ref, out_ref, acc_ref, decay_ref,
                       need_ref, qi, n_groups):
    sub = ATTN_SUB
    n_sub = q_ref.shape[1] // sub
    all_units = [(g, s) for g in range(SB_WIDTH // LANES) for s in range(n_sub)]
    per_group = len(all_units) // n_groups
    lanes = lambda g: slice(g * LANES, (g + 1) * LANES)

    def window(prev_ref, own_ref, g, s):
        if s == 0:
            return jnp.concatenate([prev_ref[0, sub:2 * sub, lanes(g)],
                                    own_ref[0, 0:sub, lanes(g)]], axis=0)
        return own_ref[0, (s - 1) * sub:(s + 1) * sub, lanes(g)]

    later_win, diff = _window_consts()
    any_need = len(all_units)
    need_ref[any_need] = 0
    groups = [all_units[k * per_group:(k + 1) * per_group] for k in range(n_groups)]
    left_valid = [[qi > 0 if s == 0 else True for _, s in units] for units in groups]
    state = [{} for _ in groups]

    def stage_scores(k):
        units, st = groups[k], state[k]
        st["z"] = [_scores(_stack_heads(q_ref[0, s * sub:(s + 1) * sub, lanes(g)]),
                           window(kp_ref, ko_ref, g, s)) for g, s in units]
        st["sp"] = [_softplus2(x) for x in st["z"]]
        spm = [_mask_window(x, diff, lv) for x, lv in zip(st["sp"], left_valid[k])]
        st["spm"] = [x.astype(BF16) for x in spm]
        for u, (g, s) in enumerate(units):
            decay = jnp.sum(spm[u], axis=1, keepdims=True)
            decay_ref[s, :, g:g + 1] = decay
            unfinished = jnp.logical_and(jnp.min(decay) < DECAY_LIMIT, qi * n_sub + s >= 2)
            need_ref[k * per_group + u] = unfinished.astype(jnp.int32)
            need_ref[any_need] = need_ref[any_need] | unfinished.astype(jnp.int32)

    def stage_later(k):
        st = state[k]
        stacked = _dot(jnp.concatenate(st["spm"], axis=0), later_win)
        later = [stacked[u * 2 * sub:(u + 1) * 2 * sub] for u in range(per_group)]
        st["a"] = [_mask_window(_weights(*zsl), diff, lv).astype(BF16)
                   for zsl, lv in zip(zip(st["z"], st["sp"], later), left_valid[k])]

    def stage_values(k):
        for a, (g, s) in zip(state[k]["a"], groups[k]):
            pv = _apply(a, window(vp_ref, vo_ref, g, s))
            acc_ref[s * sub:(s + 1) * sub, lanes(g)] = pv
            out_ref[s * sub:(s + 1) * sub, lanes(g)] = pv.astype(out_ref.dtype)

    def thunk(t):
        for stage, k in ((stage_values, t - 2), (stage_later, t - 1), (stage_scores, t)):
            if 0 <= k < n_groups:
                stage(k)

    return [functools.partial(thunk, t) for t in range(n_groups + 2)]


def _attention_sweeps(q_ref, kp_ref, ko_ref, vp_ref, vo_ref, k_any, v_any, out_ref, acc_ref,
                      decay_ref, need_ref, kbuf, vbuf, sems, b, qi):
    sub = ATTN_SUB
    n_sub = q_ref.shape[1] // sub
    later_sub = _later_matrix(sub)

    for g in range(SB_WIDTH // LANES):
        lanes = slice(g * LANES, (g + 1) * LANES)

        def fetch(first_key):
            copies = [pltpu.make_async_copy(src.at[b, pl.ds(first_key, sub), lanes], dst, sems.at[i])
                      for i, (src, dst) in enumerate(((k_any, kbuf), (v_any, vbuf)))]
            for c in copies:
                c.start()
            for c in copies:
                c.wait()

        def per_sub(s, carry):
            @pl.when(need_ref[g * n_sub + s] != 0)
            def _():
                tile = qi * n_sub + s
                rows = pl.ds(pl.multiple_of(s * sub, sub), sub)
                qs = _stack_heads(q_ref[0, rows, lanes])

                def block(kb, vb, acc, decay):
                    z = _scores(qs, kb)
                    sp = _softplus2(z)
                    a = _weights(z, sp, _later_sum(sp, later_sub) + decay)
                    return acc + _apply(a, vb), decay + jnp.sum(sp, axis=1, keepdims=True)

                def resident(prev_ref, own_ref):
                    in_own = pl.ds(pl.multiple_of(jnp.maximum(s - 2, 0) * sub, sub), sub)
                    in_prev = pl.ds(pl.multiple_of(jnp.minimum(s, 1) * sub, sub), sub)
                    return lax.cond(s >= 2, lambda: own_ref[0, in_own, lanes],
                                    lambda: prev_ref[0, in_prev, lanes])

                acc, decay = block(resident(kp_ref, ko_ref), resident(vp_ref, vo_ref),
                                   acc_ref[rows, lanes], decay_ref[s, :, g:g + 1])

                def cond(st):
                    return jnp.logical_and(st[0] >= 0, st[1])

                def body(st):
                    j, _, acc, decay = st
                    fetch(pl.multiple_of(j * sub, sub))
                    acc, decay = block(kbuf[...], vbuf[...], acc, decay)
                    return j - 1, jnp.min(decay) < DECAY_LIMIT, acc, decay

                st = lax.while_loop(cond, body,
                                    (tile - 3, jnp.min(decay) < DECAY_LIMIT, acc, decay))
                out_ref[rows, lanes] = st[2].astype(out_ref.dtype)

            return carry

        lax.fori_loop(0, n_sub, per_sub, 0)


def _post_kernel(h_ref, gm_ref, p_ref, own_ref, prev_ref, kvq_any,
                 w_mo_ref, n2_ref, w_in_ref, w_out_ref, np_ref, w_gate_ref, w_proj_ref, nf_ref,
                 o_ref, act_ref, sb_ref, acc_ref, decay_ref, need_ref, kbuf, vbuf, sems,
                 *, final, tiles_per_seq):
    j = pl.program_id(0)
    n_tiles = pl.num_programs(0) - 1
    cur = j % 2
    part = lambda ref, i: ref.at[:, :, i * SB_WIDTH:(i + 1) * SB_WIDTH]
    ko_ref, vo_ref, q_ref = (part(own_ref, i) for i in range(3))
    kp_ref, vp_ref = part(prev_ref, 0), part(prev_ref, 1)
    k_any, v_any = part(kvq_any, 0), part(kvq_any, 1)

    @pl.when(j == 0)
    def _():
        sb_ref[1] = jnp.zeros(sb_ref.shape[1:], sb_ref.dtype)

    tile = jnp.minimum(j, n_tiles - 1)
    b, qi = tile // tiles_per_seq, tile % tiles_per_seq
    attention = _attention_windows(q_ref, kp_ref, ko_ref, vp_ref, vo_ref, sb_ref.at[cur],
                                   acc_ref, decay_ref, need_ref, qi, ATTN_GROUPS)

    mixed = jnp.concatenate([gm_ref[...], sb_ref[1 - cur]], axis=1)
    h = h_ref[...] + _dot(mixed, w_mo_ref[...])
    h = h + 0.5 * _half_swiglu(h, n2_ref[...], w_in_ref, w_out_ref, act_ref, attention[:-2])
    attention[-2]()
    gate = jax.nn.sigmoid(_dot(_rms(h, np_ref[...]).astype(BF16), w_gate_ref[...]))
    attention[-1]()
    h = h + gate * _dot(p_ref[...].astype(BF16), w_proj_ref[...])
    o_ref[...] = _rms(h, nf_ref[...]) if final else h

    @pl.when(need_ref[need_ref.shape[0] - 1] != 0)
    def _():
        _attention_sweeps(q_ref, kp_ref, ko_ref, vp_ref, vo_ref, k_any, v_any, sb_ref.at[cur],
                          acc_ref, decay_ref, need_ref, kbuf, vbuf, sems, b, qi)


def _post_call(h1, gm, p2, kvq, w_mo, n2, w_in, w_out, np_, w_gate, w_proj, nf, final):
    tokens = h1.shape[0]
    batch, seq, _ = kvq.shape
    tm = TOKEN_TILE
    n_tiles = tokens // tm
    tiles_per_seq = seq // tm
    sub_per_tile = tm // ATTN_SUB
    dense = lambda width: pl.BlockSpec((tm, width), lambda j: (jnp.maximum(j - 1, 0), 0))

    def own(j):
        t = jnp.minimum(j, n_tiles - 1)
        return t // tiles_per_seq, t % tiles_per_seq, 0

    def prev(j):
        b, qi, _ = own(j)
        return b, jnp.maximum(qi * (sub_per_tile // 2) - 1, 0), 0

    own_spec = pl.BlockSpec((1, tm, 3 * SB_WIDTH), own)
    prev_spec = pl.BlockSpec((1, 2 * ATTN_SUB, 2 * SB_WIDTH), prev)
    hbm = pl.BlockSpec(memory_space=pl.ANY)
    return pl.pallas_call(
        functools.partial(_post_kernel, final=final, tiles_per_seq=tiles_per_seq),
        grid=(n_tiles + 1,),
        in_specs=[
            dense(D_MODEL), dense(GM_WIDTH), dense(PLE_DIM),
            own_spec, prev_spec, hbm,
            _resident(w_mo.shape),
            _resident((1, D_MODEL)),
            _resident(w_in.shape),
            _resident(w_out.shape),
            _resident((1, D_MODEL)),
            _resident(w_gate.shape),
            _resident(w_proj.shape),
            _resident((1, D_MODEL)),
        ],
        out_specs=dense(D_MODEL),
        out_shape=jax.ShapeDtypeStruct((tokens, D_MODEL), F32),
        scratch_shapes=[
            pltpu.VMEM((tm, D_FF), BF16),
            pltpu.VMEM((2, tm, SB_WIDTH), BF16),
            pltpu.VMEM((tm, SB_WIDTH), F32),
            pltpu.VMEM((sub_per_tile, 2 * ATTN_SUB, LANES), F32),
            pltpu.SMEM((SB_WIDTH // LANES * sub_per_tile + 1,), jnp.int32),
            pltpu.VMEM((ATTN_SUB, LANES), BF16),
            pltpu.VMEM((ATTN_SUB, LANES), BF16),
            pltpu.SemaphoreType.DMA((2,)),
        ],
        compiler_params=pltpu.CompilerParams(
            dimension_semantics=("arbitrary",), vmem_limit_bytes=VMEM_LIMIT),
        name="post_attn_mix_out_ffn_ple",
    )(h1, gm, p2, kvq, kvq, kvq, w_mo, n2, w_in, w_out, np_, w_gate, w_proj, nf)


def kernel(x, p, ffn1_norm, ffn1_w_in, ffn1_w_out, mix_norm, w_mix_in, gmlp_v_norm, gmlp_w_s, gmlp_b, w_mix_out, ffn2_norm, ffn2_w_in, ffn2_w_out, ple_norm, ple_w_gate, ple_w_proj, final_norm):
    batch, seq, _ = x.shape
    depth = p.shape[0]
    tokens = batch * seq
    vec = lambda g: g.reshape(1, -1).astype(F32)

    h = x.reshape(tokens, D_MODEL)
    for i in range(depth):
        bs_full = jnp.repeat(gmlp_b[i].T, GM_HEAD_DIM, axis=1)
        h1, gm, kvq, w_mo, w_in2, w_out2, w_gate, w_proj = _pre_call(
            h, vec(ffn1_norm[i]), ffn1_w_in, ffn1_w_out, vec(mix_norm[i]), w_mix_in,
            vec(gmlp_v_norm[i]), gmlp_w_s[i], bs_full,
            [w_mix_out, ffn2_w_in, ffn2_w_out, ple_w_gate, ple_w_proj], layer=i)
        h = _post_call(
            h1, gm, p[i].reshape(tokens, PLE_DIM), kvq.reshape(batch, seq, 3 * SB_WIDTH),
            w_mo, vec(ffn2_norm[i]), w_in2, w_out2, vec(ple_norm[i]), w_gate, w_proj,
            vec(final_norm), final=(i == depth - 1))
    return h.reshape(batch, seq, D_MODEL)
```

```python
import functools

import jax
import jax.numpy as jnp
from jax import lax
from jax.experimental import pallas as pl
from jax.experimental.pallas import tpu as pltpu

D_MODEL = 1024
PLE_DIM = 256
D_FF = 2816
CHUNK = 128
GM_HEADS = 4
GM_HEAD_DIM = 128
GM_WIDTH = GM_HEADS * GM_HEAD_DIM
SB_HEADS = 8
SB_HEAD_DIM = 64
SB_WIDTH = SB_HEADS * SB_HEAD_DIM
EPS = 1e-6

LANES = 128
FF_CHUNK = 256
TOKEN_TILE = 512
CAST_ROWS = 16
STAGE_CHUNKS = 16
ATTN_SUB = 128
ATTN_GROUPS = 8
LOG2E = 1.4426950408889634
DECAY_LIMIT = 127.0
VMEM_LIMIT = 56 * 1024 * 1024

BF16 = jnp.bfloat16
F32 = jnp.float32


def _dot(a, b):
    return jnp.dot(a, b, preferred_element_type=F32)


def _inv_rms(x):
    return lax.rsqrt(jnp.mean(x * x, axis=-1, keepdims=True) + EPS)


def _rms(x, g):
    return x * _inv_rms(x) * g


def _half_swiglu(x, g, w_in_ref, w_out_ref, act_ref, side_work=()):
    y = (x * g).astype(BF16)
    row_scale = _inv_rms(x)
    for c in range(D_FF // FF_CHUNK):
        if c < len(side_work):
            side_work[c]()
        lo = c * FF_CHUNK
        gate = _dot(y, w_in_ref[:, lo:lo + FF_CHUNK])
        up = _dot(y, w_in_ref[:, D_FF + lo:D_FF + lo + FF_CHUNK])
        gate, up = gate * row_scale, up * row_scale
        act_ref[:, lo:lo + FF_CHUNK] = (gate * jax.nn.sigmoid(gate) * up).astype(BF16)
    return _dot(act_ref[...], w_out_ref[...])


def _gelu(x):
    return 0.5 * x * (1.0 + lax.erf(x * (2.0 ** -0.5)))


def _stage_weights(weights, layer):
    def chunk(w, c):
        src, _, stage, sems = w
        rows = stage.shape[1]
        slot = c % 2
        return pltpu.make_async_copy(
            src.at[layer, pl.ds(pl.multiple_of(c * rows, rows), rows)], stage.at[slot],
            sems.at[slot])

    for w in weights:
        chunk(w, 0).start()

    def body(c, carry):
        @pl.when(c + 1 < STAGE_CHUNKS)
        def _():
            for w in weights:
                chunk(w, c + 1).start()

        for w in weights:
            _, dst, stage, _ = w
            rows = stage.shape[1]
            chunk(w, c).wait()
            dst[pl.ds(pl.multiple_of(c * rows, rows), rows), :] = stage[c % 2].astype(BF16)
        return carry

    lax.fori_loop(0, STAGE_CHUNKS, body, 0)


def _pre_kernel(x_ref, n1_ref, w_in_hbm, w_out_hbm, nm_ref, w_mix_hbm, vg_ref, ws_ref, bs_ref,
                *refs, layer):
    n_cast = (len(refs) - 13) // 2
    cast_in, refs = refs[:n_cast], refs[n_cast:]
    h_ref, gm_ref, kvq_ref = refs[:3]
    cast_out = refs[3:3 + n_cast]
    act_ref, w_in_ref, w_out_ref, w_mix_ref = refs[3 + n_cast:7 + n_cast]
    stages, sems = refs[7 + n_cast:10 + n_cast], refs[10 + n_cast:]

    @pl.when(pl.program_id(0) == 0)
    def _():
        _stage_weights(list(zip((w_in_hbm, w_out_hbm, w_mix_hbm),
                                (w_in_ref, w_out_ref, w_mix_ref), stages, sems)), layer)

    for src, dst in zip(cast_in, cast_out):
        dst[...] = src[...].astype(BF16)

    x = x_ref[...]
    h = x + 0.5 * _half_swiglu(x, n1_ref[...], w_in_ref, w_out_ref, act_ref)
    h_ref[...] = h

    n = (h * nm_ref[...]).astype(BF16)
    n_scale = _inv_rms(h)
    zu = _dot(n, w_mix_ref[:, 0:GM_WIDTH]) * n_scale
    zv = _dot(n, w_mix_ref[:, GM_WIDTH:2 * GM_WIDTH]) * n_scale

    base = 2 * GM_WIDTH
    att_q = (_dot(n, w_mix_ref[:, base:base + SB_WIDTH])
             * (n_scale * (SB_HEAD_DIM ** -0.5 * LOG2E)))
    att_k = _dot(n, w_mix_ref[:, base + SB_WIDTH:base + 2 * SB_WIDTH]) * n_scale
    att_v = _dot(n, w_mix_ref[:, base + 2 * SB_WIDTH:base + 3 * SB_WIDTH]) * n_scale
    for i, part in enumerate((att_k, att_v, att_q)):
        kvq_ref[:, i * SB_WIDTH:(i + 1) * SB_WIDTH] = part.astype(BF16)

    u = _gelu(zu)
    v = _gelu(zv)
    vn = _rms(v, vg_ref[...]).astype(BF16)

    t_idx = lax.broadcasted_iota(jnp.int32, (CHUNK, CHUNK), 0)
    s_idx = lax.broadcasted_iota(jnp.int32, (CHUNK, CHUNK), 1)
    keep = s_idx <= t_idx
    for hd in range(GM_HEADS):
        w = jnp.where(keep, ws_ref[hd], 0.0).astype(BF16)
        cols = slice(hd * GM_HEAD_DIM, (hd + 1) * GM_HEAD_DIM)
        chunks = [slice(c * CHUNK, (c + 1) * CHUNK) for c in range(x.shape[0] // CHUNK)]
        sv = _dot(w, jnp.concatenate([vn[rows, cols] for rows in chunks], axis=1))
        for rows in chunks:
            gm_ref[rows, cols] = (u[rows, cols] * (sv[:, rows] + bs_ref[:, cols])).astype(BF16)


def _resident(shape):
    return pl.BlockSpec(shape, lambda *_: (0,) * len(shape), pipeline_mode=pl.Buffered(1))


def _cast_blocks(rows, steps):
    tiles = rows // CAST_ROWS
    return max(d for d in range(1, min(steps, tiles) + 1) if tiles % d == 0)


def _stage_rows(rows):
    assert rows % (STAGE_CHUNKS * CAST_ROWS) == 0
    return rows // STAGE_CHUNKS


def _pre_call(x2, n1, w_in, w_out, nm, w_mix, vg, ws, bs_full, next_weights, layer):
    tokens = x2.shape[0]
    tm = TOKEN_TILE
    steps = tokens // tm
    row = lambda width: pl.BlockSpec((tm, width), lambda i: (i, 0))
    hbm = pl.BlockSpec(memory_space=pl.ANY)
    cast_in, cast_out = [], []
    for w in next_weights:
        _, rows, cols = w.shape
        blocks = _cast_blocks(rows, steps)
        cast_in.append(pl.BlockSpec(
            (None, rows // blocks, cols),
            lambda i, last=blocks - 1: (layer, jnp.minimum(i, last), 0)))
        cast_out.append(pl.BlockSpec(
            (rows // blocks, cols), lambda i, last=blocks - 1: (jnp.minimum(i, last), 0)))
    own = (w_in, w_out, w_mix)
    return pl.pallas_call(
        functools.partial(_pre_kernel, layer=layer),
        grid=(steps,),
        in_specs=[
            row(D_MODEL),
            _resident((1, D_MODEL)),
            hbm,
            hbm,
            _resident((1, D_MODEL)),
            hbm,
            _resident((1, GM_WIDTH)),
            _resident(ws.shape),
            _resident(bs_full.shape),
        ] + cast_in,
        out_specs=[row(D_MODEL), row(GM_WIDTH), row(3 * SB_WIDTH)] + cast_out,
        out_shape=[
            jax.ShapeDtypeStruct((tokens, D_MODEL), F32),
            jax.ShapeDtypeStruct((tokens, GM_WIDTH), BF16),
            jax.ShapeDtypeStruct((tokens, 3 * SB_WIDTH), BF16),
        ] + [jax.ShapeDtypeStruct(w.shape[1:], BF16) for w in next_weights],
        scratch_shapes=[pltpu.VMEM((tm, D_FF), BF16)]
        + [pltpu.VMEM(w.shape[1:], BF16) for w in own]
        + [pltpu.VMEM((2, _stage_rows(w.shape[1]), w.shape[2]), F32) for w in own]
        + [pltpu.SemaphoreType.DMA((2,)) for _ in own],
        compiler_params=pltpu.CompilerParams(
            dimension_semantics=("arbitrary",), vmem_limit_bytes=VMEM_LIMIT),
        name="pre_ffn_mix_in",
    )(x2, n1, w_in, w_out, nm, w_mix, vg, ws, bs_full, *next_weights)


def _later_matrix(n):
    j = lax.broadcasted_iota(jnp.int32, (n, n), 0)
    s = lax.broadcasted_iota(jnp.int32, (n, n), 1)
    return (j > s).astype(BF16)


def _stack_heads(x):
    first = lax.broadcasted_iota(jnp.int32, x.shape, 1) < SB_HEAD_DIM
    zero = jnp.zeros_like(x)
    return jnp.concatenate([jnp.where(first, x, zero), jnp.where(first, zero, x)], axis=0)


def _scores(qs, kb):
    return lax.dot_general(qs, kb, (((1,), (1,)), ((), ())), preferred_element_type=F32)


def _softplus2(z):
    return jnp.maximum(z, 0.0) + jnp.log2(1.0 + jnp.exp2(-jnp.abs(z)))


def _mask_window(x, diff, left_valid):
    left, right = x[:, :ATTN_SUB], x[:, ATTN_SUB:]
    if left_valid is not True:
        left = jnp.where(left_valid, left, 0.0)
    right = jnp.where(diff < 0, right, 0.0)
    return jnp.concatenate([left, right], axis=1)


def _later_sum(sp, later):
    return _dot(sp.astype(BF16), later)


def _weights(z, sp, decay):
    return jnp.exp2((z - sp) - decay)


def _apply(a, vb):
    pv = _dot(a.astype(BF16), vb)
    first = lax.broadcasted_iota(jnp.int32, (ATTN_SUB, LANES), 1) < SB_HEAD_DIM
    return jnp.where(first, pv[:ATTN_SUB], pv[ATTN_SUB:])


def _window_consts():
    sub = ATTN_SUB
    diff = (lax.broadcasted_iota(jnp.int32, (2 * sub, sub), 1)
            - lax.broadcasted_iota(jnp.int32, (2 * sub, sub), 0) % sub)
    return _later_matrix(2 * sub), diff


def _attention_windows(q_ref, kp_ref, ko_ref, vp_ref, vo_ref, out_ref, acc_ref, decay_ref,
                       need_ref, qi, n_groups):
    sub = ATTN_SUB
    n_sub = q_ref.shape[1] // sub
    all_units = [(g, s) for g in range(SB_WIDTH // LANES) for s in range(n_sub)]
    per_group = len(all_units) // n_groups
    lanes = lambda g: slice(g * LANES, (g + 1) * LANES)

    def window(prev_ref, own_ref, g, s):
        if s == 0:
            return jnp.concatenate([prev_ref[0, sub:2 * sub, lanes(g)],
                                    own_ref[0, 0:sub, lanes(g)]], axis=0)
        return own_ref[0, (s - 1) * sub:(s + 1) * sub, lanes(g)]

    later_win, diff = _window_consts()
    any_need = len(all_units)
    need_ref[any_need] = 0
    groups = [all_units[k * per_group:(k + 1) * per_group] for k in range(n_groups)]
    left_valid = [[qi > 0 if s == 0 else True for _, s in units] for units in groups]
    state = [{} for _ in groups]

    def stage_scores(k):
        units, st = groups[k], state[k]
        st["z"] = [_scores(_stack_heads(q_ref[0, s * sub:(s + 1) * sub, lanes(g)]),
                           window(kp_ref, ko_ref, g, s)) for g, s in units]
        st["sp"] = [_softplus2(x) for x in st["z"]]
        spm = [_mask_window(x, diff, lv) for x, lv in zip(st["sp"], left_valid[k])]
        st["spm"] = [x.astype(BF16) for x in spm]
        for u, (g, s) in enumerate(units):
            decay = jnp.sum(spm[u], axis=1, keepdims=True)
            decay_ref[s, :, g:g + 1] = decay
            unfinished = jnp.logical_and(jnp.min(decay) < DECAY_LIMIT, qi * n_sub + s >= 2)
            need_ref[k * per_group + u] = unfinished.astype(jnp.int32)
            need_ref[any_need] = need_ref[any_need] | unfinished.astype(jnp.int32)

    def stage_later(k):
        st = state[k]
        stacked = _dot(jnp.concatenate(st["spm"], axis=0), later_win)
        later = [stacked[u * 2 * sub:(u + 1) * 2 * sub] for u in range(per_group)]
        st["a"] = [_mask_window(_weights(*zsl), diff, lv).astype(BF16)
                   for zsl, lv in zip(zip(st["z"], st["sp"], later), left_valid[k])]

    def stage_values(k):
        for a, (g, s) in zip(state[k]["a"], groups[k]):
            pv = _apply(a, window(vp_ref, vo_ref, g, s))
            acc_ref[s * sub:(s + 1) * sub, lanes(g)] = pv
            out_ref[s * sub:(s + 1) * sub, lanes(g)] = pv.astype(out_ref.dtype)

    def thunk(t):
        for stage, k in ((stage_values, t - 2), (stage_later, t - 1), (stage_scores, t)):
            if 0 <= k < n_groups:
                stage(k)

    return [functools.partial(thunk, t) for t in range(n_groups + 2)]


def _attention_sweeps(q_ref, kp_ref, ko_ref, vp_ref, vo_ref, k_any, v_any, out_ref, acc_ref,
                      decay_ref, need_ref, kbuf, vbuf, sems, b, qi):
    sub = ATTN_SUB
    n_sub = q_ref.shape[1] // sub
    later_sub = _later_matrix(sub)

    for g in range(SB_WIDTH // LANES):
        lanes = slice(g * LANES, (g + 1) * LANES)

        def fetch(first_key):
            copies = [pltpu.make_async_copy(src.at[b, pl.ds(first_key, sub), lanes], dst, sems.at[i])
                      for i, (src, dst) in enumerate(((k_any, kbuf), (v_any, vbuf)))]
            for c in copies:
                c.start()
            for c in copies:
                c.wait()

        def per_sub(s, carry):
            @pl.when(need_ref[g * n_sub + s] != 0)
            def _():
                tile = qi * n_sub + s
                rows = pl.ds(pl.multiple_of(s * sub, sub), sub)
                qs = _stack_heads(q_ref[0, rows, lanes])

                def block(kb, vb, acc, decay):
                    z = _scores(qs, kb)
                    sp = _softplus2(z)
                    a = _weights(z, sp, _later_sum(sp, later_sub) + decay)
                    return acc + _apply(a, vb), decay + jnp.sum(sp, axis=1, keepdims=True)

                def resident(prev_ref, own_ref):
                    in_own = pl.ds(pl.multiple_of(jnp.maximum(s - 2, 0) * sub, sub), sub)
                    in_prev = pl.ds(pl.multiple_of(jnp.minimum(s, 1) * sub, sub), sub)
                    return lax.cond(s >= 2, lambda: own_ref[0, in_own, lanes],
                                    lambda: prev_ref[0, in_prev, lanes])

                acc, decay = block(resident(kp_ref, ko_ref), resident(vp_ref, vo_ref),
                                   acc_ref[rows, lanes], decay_ref[s, :, g:g + 1])

                def cond(st):
                    return jnp.logical_and(st[0] >= 0, st[1])

                def body(st):
                    j, _, acc, decay = st
                    fetch(pl.multiple_of(j * sub, sub))
                    acc, decay = block(kbuf[...], vbuf[...], acc, decay)
                    return j - 1, jnp.min(decay) < DECAY_LIMIT, acc, decay

                st = lax.while_loop(cond, body,
                                    (tile - 3, jnp.min(decay) < DECAY_LIMIT, acc, decay))
                out_ref[rows, lanes] = st[2].astype(out_ref.dtype)

            return carry

        lax.fori_loop(0, n_sub, per_sub, 0)


def _post_kernel(h_ref, gm_ref, p_ref, own_ref, prev_ref, kvq_any,
                 w_mo_ref, n2_ref, w_in_ref, w_out_ref, np_ref, w_gate_ref, w_proj_ref, nf_ref,
                 o_ref, act_ref, sb_ref, acc_ref, decay_ref, need_ref, kbuf, vbuf, sems,
                 *, final, tiles_per_seq):
    j = pl.program_id(0)
    n_tiles = pl.num_programs(0) - 1
    cur = j % 2
    part = lambda ref, i: ref.at[:, :, i * SB_WIDTH:(i + 1) * SB_WIDTH]
    ko_ref, vo_ref, q_ref = (part(own_ref, i) for i in range(3))
    kp_ref, vp_ref = part(prev_ref, 0), part(prev_ref, 1)
    k_any, v_any = part(kvq_any, 0), part(kvq_any, 1)

    @pl.when(j == 0)
    def _():
        sb_ref[1] = jnp.zeros(sb_ref.shape[1:], sb_ref.dtype)

    tile = jnp.minimum(j, n_tiles - 1)
    b, qi = tile // tiles_per_seq, tile % tiles_per_seq
    attention = _attention_windows(q_ref, kp_ref, ko_ref, vp_ref, vo_ref, sb_ref.at[cur],
                                   acc_ref, decay_ref, need_ref, qi, ATTN_GROUPS)

    mixed = jnp.concatenate([gm_ref[...], sb_ref[1 - cur]], axis=1)
    h = h_ref[...] + _dot(mixed, w_mo_ref[...])
    h = h + 0.5 * _half_swiglu(h, n2_ref[...], w_in_ref, w_out_ref, act_ref, attention[:-2])
    attention[-2]()
    gate = jax.nn.sigmoid(
        _dot((h * np_ref[...]).astype(BF16), w_gate_ref[...]) * _inv_rms(h))
    attention[-1]()
    h = h + gate * _dot(p_ref[...].astype(BF16), w_proj_ref[...])
    o_ref[...] = _rms(h, nf_ref[...]) if final else h

    @pl.when(need_ref[need_ref.shape[0] - 1] != 0)
    def _():
        _attention_sweeps(q_ref, kp_ref, ko_ref, vp_ref, vo_ref, k_any, v_any, sb_ref.at[cur],
                          acc_ref, decay_ref, need_ref, kbuf, vbuf, sems, b, qi)


def _post_call(h1, gm, p2, kvq, w_mo, n2, w_in, w_out, np_, w_gate, w_proj, nf, final):
    tokens = h1.shape[0]
    batch, seq, _ = kvq.shape
    tm = TOKEN_TILE
    n_tiles = tokens // tm
    tiles_per_seq = seq // tm
    sub_per_tile = tm // ATTN_SUB
    dense = lambda width: pl.BlockSpec((tm, width), lambda j: (jnp.maximum(j - 1, 0), 0))

    def own(j):
        t = jnp.minimum(j, n_tiles - 1)
        return t // tiles_per_seq, t % tiles_per_seq, 0

    def prev(j):
        b, qi, _ = own(j)
        return b, jnp.maximum(qi * (sub_per_tile // 2) - 1, 0), 0

    own_spec = pl.BlockSpec((1, tm, 3 * SB_WIDTH), own)
    prev_spec = pl.BlockSpec((1, 2 * ATTN_SUB, 2 * SB_WIDTH), prev)
    hbm = pl.BlockSpec(memory_space=pl.ANY)
    return pl.pallas_call(
        functools.partial(_post_kernel, final=final, tiles_per_seq=tiles_per_seq),
        grid=(n_tiles + 1,),
        in_specs=[
            dense(D_MODEL), dense(GM_WIDTH), dense(PLE_DIM),
            own_spec, prev_spec, hbm,
            _resident(w_mo.shape),
            _resident((1, D_MODEL)),
            _resident(w_in.shape),
            _resident(w_out.shape),
            _resident((1, D_MODEL)),
            _resident(w_gate.shape),
            _resident(w_proj.shape),
            _resident((1, D_MODEL)),
        ],
        out_specs=dense(D_MODEL),
        out_shape=jax.ShapeDtypeStruct((tokens, D_MODEL), F32),
        scratch_shapes=[
            pltpu.VMEM((tm, D_FF), BF16),
            pltpu.VMEM((2, tm, SB_WIDTH), BF16),
            pltpu.VMEM((tm, SB_WIDTH), F32),
            pltpu.VMEM((sub_per_tile, 2 * ATTN_SUB, LANES), F32),
            pltpu.SMEM((SB_WIDTH // LANES * sub_per_tile + 1,), jnp.int32),
            pltpu.VMEM((ATTN_SUB, LANES), BF16),
            pltpu.VMEM((ATTN_SUB, LANES), BF16),
            pltpu.SemaphoreType.DMA((2,)),
        ],
        compiler_params=pltpu.CompilerParams(
            dimension_semantics=("arbitrary",), vmem_limit_bytes=VMEM_LIMIT),
        name="post_attn_mix_out_ffn_ple",
    )(h1, gm, p2, kvq, kvq, kvq, w_mo, n2, w_in, w_out, np_, w_gate, w_proj, nf)


def kernel(x, p, ffn1_norm, ffn1_w_in, ffn1_w_out, mix_norm, w_mix_in, gmlp_v_norm, gmlp_w_s, gmlp_b, w_mix_out, ffn2_norm, ffn2_w_in, ffn2_w_out, ple_norm, ple_w_gate, ple_w_proj, final_norm):
    batch, seq, _ = x.shape
    depth = p.shape[0]
    tokens = batch * seq
    vec = lambda g: g.reshape(1, -1).astype(F32)

    h = x.reshape(tokens, D_MODEL)
    for i in range(depth):
        bs_full = jnp.repeat(gmlp_b[i].T, GM_HEAD_DIM, axis=1)
        h1, gm, kvq, w_mo, w_in2, w_out2, w_gate, w_proj = _pre_call(
            h, vec(ffn1_norm[i]), ffn1_w_in, ffn1_w_out, vec(mix_norm[i]), w_mix_in,
            vec(gmlp_v_norm[i]), gmlp_w_s[i], bs_full,
            [w_mix_out, ffn2_w_in, ffn2_w_out, ple_w_gate, ple_w_proj], layer=i)
        h = _post_call(
            h1, gm, p[i].reshape(tokens, PLE_DIM), kvq.reshape(batch, seq, 3 * SB_WIDTH),
            w_mo, vec(ffn2_norm[i]), w_in2, w_out2, vec(ple_norm[i]), w_gate, w_proj,
            vec(final_norm), final=(i == depth - 1))
    return h.reshape(batch, seq, D_MODEL)
```
